```python
import jax, jax.numpy as jnp
from jax import lax
import numpy as np


D_MODEL = 4096
BATCH = 4
SEQ = 4096
DEPTH = 4
DEC_BATCH = 4
DEC_SEQ = 2048
PAST_LEN = 128

GRID_W = 64
N_HEADS = 32
HEAD_DIM = D_MODEL // N_HEADS
N_FOURIER_GROUPS = 32
FOURIER_GROUP = D_MODEL // N_FOURIER_GROUPS
WIN_ROWS_MAX = 8
WIN_COLS = 16
D_FF = 4 * D_MODEL
N_MIXERS = 2
N_FOURIER_LAYERS = (DEPTH + 1) // 2
N_NA_LAYERS = DEPTH // 2
EPS = 1e-6
NEG_INF = -1e30

kernel_name = 'fnet_natten_interleaved_encoder'


def rms_norm(x, g):
    xf = x.astype(jnp.float32)
    y = xf * lax.rsqrt(jnp.mean(xf * xf, axis=-1, keepdims=True) + EPS)
    return (y * g.astype(jnp.float32)).astype(x.dtype)


def fourier_mixer(h, w_o):
    b, s, d = h.shape
    hg = h.astype(jnp.float32).reshape(b, s, N_FOURIER_GROUPS, FOURIER_GROUP)
    f = jnp.fft.fft2(hg, axes=(1, 3), norm='ortho')
    mixed = jnp.real(f).astype(h.dtype).reshape(b, s, d)
    return mixed @ w_o


def _column_blocks():
    n_cb = GRID_W // WIN_COLS
    kb = 2 * WIN_COLS
    band_start = np.clip(np.arange(n_cb) * WIN_COLS - WIN_COLS // 2, 0, GRID_W - kb)
    key_cols = band_start[:, None] + np.arange(kb)[None, :]
    q_cols = np.arange(GRID_W).reshape(n_cb, WIN_COLS)
    win_start = np.clip(q_cols - WIN_COLS // 2, 0, GRID_W - WIN_COLS)
    kc = key_cols[:, None, :]
    valid = (kc >= win_start[..., None]) & (kc < win_start[..., None] + WIN_COLS)
    col_off = np.clip(kc - q_cols[..., None] + WIN_COLS - 1, 0, 2 * WIN_COLS - 2)
    return key_cols.astype(np.int32), valid, col_off.astype(np.int32)


def neighborhood_attention(h, w_qkv, g_q, g_k, rpb, w_o):
    b, s, d = h.shape
    rows = s // GRID_W
    kh = min(WIN_ROWS_MAX, rows)
    key_cols, valid, col_off = _column_blocks()
    n_cb, qb, kb = valid.shape
    key_cols = jnp.asarray(key_cols)
    col_off = jnp.asarray(col_off)
    mask = jnp.asarray(valid)[None, None, :, :, None, :]
    scale = HEAD_DIM ** -0.5

    qkv = (h @ w_qkv).reshape(b, rows, GRID_W, 3, N_HEADS, HEAD_DIM)
    q = rms_norm(qkv[:, :, :, 0], g_q).reshape(b, rows, n_cb, qb, N_HEADS, HEAD_DIM)
    k = rms_norm(qkv[:, :, :, 1], g_k)
    v = qkv[:, :, :, 2]

    def one_row(r):
        r0 = jnp.clip(r - kh // 2, 0, rows - kh)
        k_rows = lax.dynamic_slice_in_dim(k, r0, kh, axis=1)
        v_rows = lax.dynamic_slice_in_dim(v, r0, kh, axis=1)
        k_blk = k_rows[:, :, key_cols]
        v_blk = v_rows[:, :, key_cols]
        q_r = lax.dynamic_index_in_dim(q, r, axis=1, keepdims=False)
        scores = jnp.einsum('bnqhd,binkhd->bhnqik', q_r, k_blk).astype(jnp.float32) * scale
        row_off = r0 + jnp.arange(kh) - r + WIN_ROWS_MAX - 1
        bias = rpb[:, row_off][:, :, col_off]
        bias = bias.transpose(0, 2, 3, 1, 4).astype(jnp.float32)
        scores = jnp.where(mask, scores + bias[None], NEG_INF)
        sh = scores.shape
        p = jax.nn.softmax(scores.reshape(sh[0], sh[1], sh[2], sh[3], -1), axis=-1).reshape(sh)
        return jnp.einsum('bhnqik,binkhd->bnqhd', p.astype(v_blk.dtype), v_blk)

    out = lax.map(one_row, jnp.arange(rows))
    out = out.transpose(1, 0, 2, 3, 4, 5).reshape(b, s, d)
    return out @ w_o


def squared_relu_mlp(h, w_in, w_out):
    return jnp.square(jax.nn.relu(h @ w_in)) @ w_out


def _trunk(x, norm_mix, norm_ffn, fourier_w_o, na_w_qkv, na_g_q, na_g_k, na_rpb, na_w_o,
           ffn_w_in, ffn_w_out):
    for i in range(DEPTH):
        h = rms_norm(x, norm_mix[i])
        j = i // N_MIXERS
        if i % N_MIXERS == 0:
            x = x + fourier_mixer(h, fourier_w_o[j])
        else:
            x = x + neighborhood_attention(h, na_w_qkv[j], na_g_q[j], na_g_k[j], na_rpb[j], na_w_o[j])
        h = rms_norm(x, norm_ffn[i])
        x = x + squared_relu_mlp(h, ffn_w_in[i], ffn_w_out[i])
    return x


def setup_inputs(seed: int = 0) -> dict:
    key = jax.random.key(seed)
    ks = jax.random.split(key, 13)
    f32 = jnp.float32
    n_rel_r = 2 * WIN_ROWS_MAX - 1
    n_rel_c = 2 * WIN_COLS - 1
    return {
        'x_prompt': jax.random.normal(ks[0], (BATCH, SEQ, D_MODEL), f32),
        'x_sample': jax.random.normal(ks[1], (DEC_BATCH, DEC_SEQ, D_MODEL), f32),
        'norm_mix': 1.0 + 0.01 * jax.random.normal(ks[2], (DEPTH, D_MODEL), f32),
        'norm_ffn': 1.0 + 0.01 * jax.random.normal(ks[3], (DEPTH, D_MODEL), f32),
        'fourier_w_o': jax.random.normal(ks[4], (N_FOURIER_LAYERS, D_MODEL, D_MODEL), f32) * D_MODEL ** -0.5,
        'na_w_qkv': jax.random.normal(ks[5], (N_NA_LAYERS, D_MODEL, 3 * D_MODEL), f32) * D_MODEL ** -0.5,
        'na_g_q': 1.0 + 0.01 * jax.random.normal(ks[6], (N_NA_LAYERS, HEAD_DIM), f32),
        'na_g_k': 1.0 + 0.01 * jax.random.normal(ks[7], (N_NA_LAYERS, HEAD_DIM), f32),
        'na_rpb': 0.02 * jax.random.normal(ks[8], (N_NA_LAYERS, N_HEADS, n_rel_r, n_rel_c), f32),
        'na_w_o': jax.random.normal(ks[9], (N_NA_LAYERS, D_MODEL, D_MODEL), f32) * D_MODEL ** -0.5,
        'ffn_w_in': jax.random.normal(ks[10], (DEPTH, D_MODEL, D_FF), f32) * D_MODEL ** -0.5,
        'ffn_w_out': jax.random.normal(ks[11], (DEPTH, D_FF, D_MODEL), f32) * D_FF ** -0.5,
    }


def reference(x_prompt, x_sample, norm_mix, norm_ffn, fourier_w_o, na_w_qkv, na_g_q, na_g_k,
              na_rpb, na_w_o, ffn_w_in, ffn_w_out):
    y_prompt = _trunk(x_prompt, norm_mix, norm_ffn, fourier_w_o, na_w_qkv, na_g_q, na_g_k,
                      na_rpb, na_w_o, ffn_w_in, ffn_w_out)
    y_sample = _trunk(x_sample, norm_mix, norm_ffn, fourier_w_o, na_w_qkv, na_g_q, na_g_k,
                      na_rpb, na_w_o, ffn_w_in, ffn_w_out)
    return (y_prompt, y_sample)
```

```python
import functools

import jax
import jax.numpy as jnp
from jax import lax
from jax.experimental import pallas as pl
from jax.experimental.pallas import tpu as pltpu

GRID_W = 64
HEAD_DIM = 128
FOURIER_GROUP = 128
WIN_ROWS = 8
WIN_COLS = 16
N_MIXERS = 2
EPS = 1e-6
NEG_INF = -1e30

V7X_VMEM_BYTES = 64 * 1024 * 1024
VMEM_LIMIT_BYTES = V7X_VMEM_BYTES - 8 * 1024 * 1024
LANES = 128

F32 = jnp.float32
BF16 = jnp.bfloat16


def _params(semantics):
    return pltpu.CompilerParams(dimension_semantics=semantics, vmem_limit_bytes=VMEM_LIMIT_BYTES)


def _rms_scale(x):
    return lax.rsqrt(jnp.mean(x * x, axis=-1, keepdims=True) + EPS)


def _rmsnorm_kernel(x_ref, g_ref, o_ref):
    x = x_ref[...]
    o_ref[...] = (x * _rms_scale(x) * g_ref[...]).astype(o_ref.dtype)


def _rmsnorm(x, g, *, bm=256):
    t, d = x.shape
    bm = min(bm, t)
    return pl.pallas_call(
        _rmsnorm_kernel,
        grid=(t // bm,),
        in_specs=[pl.BlockSpec((bm, d), lambda i: (i, 0)), pl.BlockSpec((1, d), lambda i: (0, 0))],
        out_specs=pl.BlockSpec((bm, d), lambda i: (i, 0)),
        out_shape=jax.ShapeDtypeStruct((t, d), BF16),
        compiler_params=_params(("parallel",)),
        name="rmsnorm",
    )(x, g.reshape(1, d))


def _mm_kernel(*refs, nk, epilogue, n_norm_tiles):
    a_ref, w_ref = refs[0], refs[1]
    extra_ref = refs[2] if epilogue in ("residual", "headnorm") else None
    o_ref = refs[3] if extra_ref is not None else refs[2]
    acc_ref = refs[-1] if nk > 1 else None
    j = pl.program_id(2)
    k = pl.program_id(3)

    def finish(acc):
        if epilogue == "cast":
            o_ref[...] = acc.astype(o_ref.dtype)
        elif epilogue == "residual":
            o_ref[...] = (extra_ref[...] + acc).astype(o_ref.dtype)
        else:
            @pl.when(j < n_norm_tiles)
            def _():
                for h in range(acc.shape[1] // HEAD_DIM):
                    sl = slice(h * HEAD_DIM, (h + 1) * HEAD_DIM)
                    a = acc[:, sl]
                    o_ref[:, sl] = (a * _rms_scale(a) * extra_ref[:, sl]).astype(o_ref.dtype)

            @pl.when(j >= n_norm_tiles)
            def _():
                o_ref[...] = acc.astype(o_ref.dtype)

    part = jnp.dot(a_ref[...], w_ref[...], preferred_element_type=F32)
    if nk == 1:
        finish(part)
    else:
        @pl.when(k == 0)
        def _():
            acc_ref[...] = part

        @pl.when(jnp.logical_and(k > 0, k < nk - 1))
        def _():
            acc_ref[...] += part

        @pl.when(k == nk - 1)
        def _():
            finish(acc_ref[...] + part)


def _matmul(a, w, *, bm, bn, bk, out_dtype, epilogue="cast", extra=None, n_norm_tiles=0, name):
    ba, m, kdim = a.shape
    bw, _, n = w.shape
    nb = max(ba, bw)
    bm, bn, bk = min(bm, m), min(bn, n), min(bk, kdim)
    nk = kdim // bk
    a_map = (lambda b, i, j, k: (b, i, k)) if ba > 1 else (lambda b, i, j, k: (0, i, k))
    w_map = (lambda b, i, j, k: (b, k, j)) if bw > 1 else (lambda b, i, j, k: (0, k, j))
    in_specs = [pl.BlockSpec((None, bm, bk), a_map), pl.BlockSpec((None, bk, bn), w_map)]
    args = [a, w]
    if epilogue == "residual":
        in_specs.append(pl.BlockSpec((None, bm, bn), lambda b, i, j, k: (b, i, j)))
        args.append(extra)
    elif epilogue == "headnorm":
        in_specs.append(pl.BlockSpec((1, bn), lambda b, i, j, k: (0, j)))
        args.append(extra)
    return pl.pallas_call(
        functools.partial(_mm_kernel, nk=nk, epilogue=epilogue, n_norm_tiles=n_norm_tiles),
        grid=(nb, m // bm, n // bn, nk),
        in_specs=in_specs,
        out_specs=pl.BlockSpec((None, bm, bn), lambda b, i, j, k: (b, i, j)),
        out_shape=jax.ShapeDtypeStruct((nb, m, n), out_dtype),
        scratch_shapes=[pltpu.VMEM((bm, bn), F32)] if nk > 1 else [],
        compiler_params=_params(("parallel", "parallel", "parallel", "arbitrary")),
        name=name,
    )(*args)


def _chan_dft_kernel(x_ref, g_ref, cs_ref, z_ref):
    x = x_ref[...]
    h = (x * _rms_scale(x) * g_ref[...]).astype(BF16)
    for grp in range(x.shape[1] // FOURIER_GROUP):
        sl = slice(grp * FOURIER_GROUP, (grp + 1) * FOURIER_GROUP)
        y = jnp.dot(h[:, sl], cs_ref[...], preferred_element_type=F32)
        z_ref[0, :, sl] = y[:, :FOURIER_GROUP].astype(z_ref.dtype)
        z_ref[1, :, sl] = y[:, FOURIER_GROUP:].astype(z_ref.dtype)


def _chan_dft(x, g, cs, *, bm=256):
    b, s, d = x.shape
    bm = min(bm, s)
    return pl.pallas_call(
        _chan_dft_kernel,
        grid=(b, s // bm),
        in_specs=[
            pl.BlockSpec((None, bm, d), lambda bi, i: (bi, i, 0)),
            pl.BlockSpec((1, d), lambda bi, i: (0, 0)),
            pl.BlockSpec((FOURIER_GROUP, 2 * FOURIER_GROUP), lambda bi, i: (0, 0)),
        ],
        out_specs=pl.BlockSpec((None, 2, bm, d), lambda bi, i: (bi, 0, i, 0)),
        out_shape=jax.ShapeDtypeStruct((b, 2, s, d), BF16),
        compiler_params=_params(("parallel", "parallel")),
        name="chan_dft",
    )(x, g.reshape(1, d), cs)


def _dft_tables(s):
    def basis(n):
        idx = lax.broadcasted_iota(jnp.int32, (n, n), 0) * lax.broadcasted_iota(jnp.int32, (n, n), 1)
        ang = (idx % n).astype(F32) * (2.0 * jnp.pi / n)
        scale = n ** -0.5
        return jnp.cos(ang) * scale, jnp.sin(ang) * scale

    cc, sc = basis(FOURIER_GROUP)
    cq, sq = basis(s)
    return jnp.concatenate([cc, sc], axis=1).astype(BF16), jnp.concatenate([cq, -sq], axis=1).astype(BF16)


def _attn_kernel(q_ref, k_ref, v_ref, bias_ref, o_ref, *, rows):
    n_keys = WIN_ROWS * GRID_W
    q_col = lax.broadcasted_iota(jnp.int32, (GRID_W, n_keys), 0)
    k_col = lax.broadcasted_iota(jnp.int32, (GRID_W, n_keys), 1) & (GRID_W - 1)
    win_start = jnp.clip(q_col - WIN_COLS // 2, 0, GRID_W - WIN_COLS)
    valid = jnp.logical_and(k_col >= win_start, k_col < win_start + WIN_COLS)

    def body(r, carry):
        r0 = jnp.clip(r - WIN_ROWS // 2, 0, rows - WIN_ROWS)
        q = q_ref[pl.ds(pl.multiple_of(r * GRID_W, GRID_W), GRID_W), :]
        k = k_ref[pl.ds(pl.multiple_of(r0 * GRID_W, GRID_W), n_keys), :]
        v = v_ref[pl.ds(pl.multiple_of(r0 * GRID_W, GRID_W), n_keys), :]
        s = lax.dot_general(q, k, (((1,), (1,)), ((), ())), preferred_element_type=F32)
        s = jnp.where(valid, s + bias_ref[r - r0], NEG_INF)
        p = jnp.exp(s - jnp.max(s, axis=-1, keepdims=True))
        denom = jnp.sum(p, axis=-1, keepdims=True)
        o = jnp.dot(p.astype(BF16), v, preferred_element_type=F32)
        o_ref[pl.ds(pl.multiple_of(r * GRID_W, GRID_W), GRID_W), :] = (o / denom).astype(o_ref.dtype)
        return carry

    lax.fori_loop(0, rows, body, 0)


def _attention(qkv, bias):
    b, s, d3 = qkv.shape
    d = d3 // 3
    n_heads = d // HEAD_DIM
    rows = s // GRID_W
    n_keys = WIN_ROWS * GRID_W
    return pl.pallas_call(
        functools.partial(_attn_kernel, rows=rows),
        grid=(n_heads, b),
        in_specs=[
            pl.BlockSpec((None, s, HEAD_DIM), lambda h, bi: (bi, 0, h)),
            pl.BlockSpec((None, s, HEAD_DIM), lambda h, bi: (bi, 0, n_heads + h)),
            pl.BlockSpec((None, s, HEAD_DIM), lambda h, bi: (bi, 0, 2 * n_heads + h)),
            pl.BlockSpec((None, WIN_ROWS, GRID_W, n_keys), lambda h, bi: (h, 0, 0, 0)),
        ],
        out_specs=pl.BlockSpec((None, s, HEAD_DIM), lambda h, bi: (bi, 0, h)),
        out_shape=jax.ShapeDtypeStruct((b, s, d), BF16),
        compiler_params=_params(("parallel", "parallel")),
        name="nbr_attention",
    )(qkv, qkv, qkv, bias)


def _bias_table(rpb):
    col = jnp.arange(GRID_W)
    col_off = jnp.clip(col[None, :] - col[:, None] + WIN_COLS - 1, 0, 2 * WIN_COLS - 2)
    win = jnp.arange(WIN_ROWS)
    row_off = win[None, :] - win[:, None] + WIN_ROWS - 1
    t = rpb[:, row_off][:, :, :, col_off]
    return t.transpose(0, 1, 3, 2, 4).reshape(rpb.shape[0], WIN_ROWS, GRID_W, WIN_ROWS * GRID_W)


def _mlp_kernel(h_ref, x_ref, win_ref, wout_ref, o_ref):
    @pl.when(pl.program_id(1) == 0)
    def _():
        o_ref[...] = x_ref[...]

    u = jnp.maximum(jnp.dot(h_ref[...], win_ref[...], preferred_element_type=F32), 0.0)
    o_ref[...] += jnp.dot((u * u).astype(BF16), wout_ref[...], preferred_element_type=F32)


def _mlp(h, x, w_in, w_out, *, bm=512, bf=512):
    t, d = x.shape
    f = w_in.shape[1]
    bm, bf = min(bm, t), min(bf, f)
    return pl.pallas_call(
        _mlp_kernel,
        grid=(t // bm, f // bf),
        in_specs=[
            pl.BlockSpec((bm, d), lambda i, j: (i, 0)),
            pl.BlockSpec((bm, d), lambda i, j: (i, 0), pipeline_mode=pl.Buffered(1)),
            pl.BlockSpec((d, bf), lambda i, j: (0, j)),
            pl.BlockSpec((bf, d), lambda i, j: (j, 0)),
        ],
        out_specs=pl.BlockSpec((bm, d), lambda i, j: (i, 0)),
        out_shape=jax.ShapeDtypeStruct((t, d), F32),
        compiler_params=_params(("parallel", "arbitrary")),
        name="relu2_mlp",
    )(h, x, w_in, w_out)


def _trunk(x, w):
    b, s, d = x.shape
    t = b * s
    n_heads = d // HEAD_DIM
    depth = w["norm_mix"].shape[0]
    cs, seq = _dft_tables(s)
    for layer in range(depth):
        j = layer // N_MIXERS
        if layer % N_MIXERS == 0:
            z = _chan_dft(x, w["norm_mix"][layer], cs)
            mixed = _matmul(seq[None], z.reshape(b, 2 * s, d), bm=1024, bn=1024, bk=2048,
                            out_dtype=BF16, name="seq_dft")
            mix_in, w_o = mixed, w["fourier_w_o"][j]
        else:
            h = _rmsnorm(x.reshape(t, d), w["norm_mix"][layer])
            bn_qkv = min(1024, d)
            qkv = _matmul(h[None], w["na_w_qkv"][j][None], bm=1024, bn=bn_qkv, bk=d, out_dtype=BF16,
                          epilogue="headnorm", extra=w["na_qk_gain"][j], n_norm_tiles=2 * d // bn_qkv,
                          name="qkv_proj")
            mix_in, w_o = _attention(qkv.reshape(b, s, 3 * d), w["na_bias"][j]), w["na_w_o"][j]
        x = _matmul(mix_in.reshape(1, t, d), w_o[None], bm=1024, bn=512, bk=d, out_dtype=F32,
                    epilogue="residual", extra=x.reshape(1, t, d), name="out_proj").reshape(b, s, d)
        h = _rmsnorm(x.reshape(t, d), w["norm_ffn"][layer])
        x = _mlp(h, x.reshape(t, d), w["ffn_w_in"][layer], w["ffn_w_out"][layer]).reshape(b, s, d)
    return x


def kernel(x_prompt, x_sample, norm_mix, norm_ffn, fourier_w_o, na_w_qkv, na_g_q, na_g_k, na_rpb, na_w_o,
           ffn_w_in, ffn_w_out):
    d = x_prompt.shape[-1]
    n_heads = d // HEAD_DIM
    n_na = na_w_qkv.shape[0]
    qk_gain = jnp.concatenate(
        [jnp.tile(na_g_q * HEAD_DIM ** -0.5, (1, n_heads)), jnp.tile(na_g_k, (1, n_heads)),
         jnp.ones((n_na, d), F32)], axis=1).reshape(n_na, 1, 3 * d)
    w = {
        "norm_mix": norm_mix,
        "norm_ffn": norm_ffn,
        "fourier_w_o": fourier_w_o.astype(BF16),
        "na_w_qkv": na_w_qkv.astype(BF16),
        "na_qk_gain": qk_gain,
        "na_bias": jax.vmap(_bias_table)(na_rpb),
        "na_w_o": na_w_o.astype(BF16),
        "ffn_w_in": ffn_w_in.astype(BF16),
        "ffn_w_out": ffn_w_out.astype(BF16),
    }
    return _trunk(x_prompt, w), _trunk(x_sample, w)
```

```python
import functools
import math

import numpy as np
import jax
import jax.numpy as jnp
from jax import lax
from jax.experimental import pallas as pl
from jax.experimental.pallas import tpu as pltpu

GRID_W = 64
HEAD_DIM = 128
FOURIER_GROUP = 128
WIN_ROWS = 8
WIN_COLS = 16
N_MIXERS = 2
DFT_RADIX = 4
EPS = 1e-6
NEG_INF = -1e30

V7X_VMEM_BYTES = 64 * 1024 * 1024
VMEM_LIMIT_BYTES = V7X_VMEM_BYTES - 8 * 1024 * 1024
LANES = 128
ATTN_Q_ROWS = 4
ATTN_K_TILES = 3
ATTN_GROUPS_PER_STEP = 2
ATTN_HEADS_PER_STEP = 2

F32 = jnp.float32
BF16 = jnp.bfloat16


def _params(semantics):
    return pltpu.CompilerParams(dimension_semantics=semantics, vmem_limit_bytes=VMEM_LIMIT_BYTES)


def _rms_scale(x):
    return lax.rsqrt(jnp.mean(x * x, axis=-1, keepdims=True) + EPS)


def _rmsnorm_kernel(x_ref, g_ref, o_ref):
    x = x_ref[...]
    o_ref[...] = (x * _rms_scale(x) * g_ref[...]).astype(o_ref.dtype)


def _rmsnorm(x, g, *, bm=256):
    t, d = x.shape
    bm = min(bm, t)
    return pl.pallas_call(
        _rmsnorm_kernel,
        grid=(t // bm,),
        in_specs=[pl.BlockSpec((bm, d), lambda i: (i, 0)), pl.BlockSpec((1, d), lambda i: (0, 0))],
        out_specs=pl.BlockSpec((bm, d), lambda i: (i, 0)),
        out_shape=jax.ShapeDtypeStruct((t, d), BF16),
        compiler_params=_params(("parallel",)),
        name="rmsnorm",
    )(x, g.reshape(1, d))


def _mm_kernel(*refs, nk, epilogue, n_norm_tiles):
    a_ref, w_ref = refs[0], refs[1]
    extra_ref = refs[2] if epilogue in ("residual", "headnorm") else None
    o_ref = refs[3] if extra_ref is not None else refs[2]
    acc_ref = refs[-1] if nk > 1 else None
    j = pl.program_id(2)
    k = pl.program_id(3)

    def finish(acc):
        if epilogue == "cast":
            o_ref[...] = acc.astype(o_ref.dtype)
        elif epilogue == "residual":
            o_ref[...] = (extra_ref[...] + acc).astype(o_ref.dtype)
        else:
            @pl.when(j < n_norm_tiles)
            def _():
                for h in range(acc.shape[1] // HEAD_DIM):
                    sl = slice(h * HEAD_DIM, (h + 1) * HEAD_DIM)
                    a = acc[:, sl]
                    o_ref[:, sl] = (a * _rms_scale(a) * extra_ref[:, sl]).astype(o_ref.dtype)

            @pl.when(j >= n_norm_tiles)
            def _():
                o_ref[...] = acc.astype(o_ref.dtype)

    part = jnp.dot(a_ref[...], w_ref[...], preferred_element_type=F32)
    if nk == 1:
        finish(part)
    else:
        @pl.when(k == 0)
        def _():
            acc_ref[...] = part

        @pl.when(jnp.logical_and(k > 0, k < nk - 1))
        def _():
            acc_ref[...] += part

        @pl.when(k == nk - 1)
        def _():
            finish(acc_ref[...] + part)


def _matmul(a, w, *, bm, bn, bk, out_dtype, epilogue="cast", extra=None, n_norm_tiles=0, w_layer=None,
            extra_layer=None, name):
    ba, m, kdim = a.shape
    bw, _, n = w.shape
    if w_layer is not None:
        bw = 1
    nb = max(ba, bw)
    bm, bn, bk = min(bm, m), min(bn, n), min(bk, kdim)
    nk = kdim // bk
    a_map = (lambda b, i, j, k: (b, i, k)) if ba > 1 else (lambda b, i, j, k: (0, i, k))
    w_first = 0 if w_layer is None else w_layer
    w_map = (lambda b, i, j, k: (b, k, j)) if bw > 1 else (lambda b, i, j, k: (w_first, k, j))
    in_specs = [pl.BlockSpec((None, bm, bk), a_map), pl.BlockSpec((None, bk, bn), w_map)]
    args = [a, w]
    if epilogue == "residual":
        in_specs.append(pl.BlockSpec((None, bm, bn), lambda b, i, j, k: (b, i, j)))
        args.append(extra)
    elif epilogue == "headnorm":
        in_specs.append(pl.BlockSpec((None, 1, bn), lambda b, i, j, k: (extra_layer, 0, j)))
        args.append(extra)
    return pl.pallas_call(
        functools.partial(_mm_kernel, nk=nk, epilogue=epilogue, n_norm_tiles=n_norm_tiles),
        grid=(nb, m // bm, n // bn, nk),
        in_specs=in_specs,
        out_specs=pl.BlockSpec((None, bm, bn), lambda b, i, j, k: (b, i, j)),
        out_shape=jax.ShapeDtypeStruct((nb, m, n), out_dtype),
        scratch_shapes=[pltpu.VMEM((bm, bn), F32)] if nk > 1 else [],
        compiler_params=_params(("parallel", "parallel", "parallel", "arbitrary")),
        name=name,
    )(*args)


def _chan_dft_kernel(x0_ref, x1_ref, x2_ref, x3_ref, g_ref, cs_ref, z_ref):
    h0, h1, h2, h3 = ((x_ref[...] * _rms_scale(x_ref[...]) * g_ref[...]) for x_ref in (x0_ref, x1_ref, x2_ref, x3_ref))
    even, odd = h0 + h2, h1 + h3
    lhs = jnp.concatenate([even + odd, even - odd, h0 - h2, h1 - h3], axis=0).astype(BF16)
    bm = h0.shape[0]
    for grp in range(h0.shape[1] // FOURIER_GROUP):
        sl = slice(grp * FOURIER_GROUP, (grp + 1) * FOURIER_GROUP)
        y = jnp.dot(lhs[:, sl], cs_ref[...], preferred_element_type=F32)
        yc, ys = y[:, :FOURIER_GROUP], y[:, FOURIER_GROUP:]
        ac, a_s, bc, b_s = yc[2 * bm:3 * bm], ys[2 * bm:3 * bm], yc[3 * bm:], ys[3 * bm:]
        z_ref[0, 0, :, sl] = yc[:bm].astype(z_ref.dtype)
        z_ref[0, 1, :, sl] = ys[:bm].astype(z_ref.dtype)
        z_ref[2, 0, :, sl] = yc[bm:2 * bm].astype(z_ref.dtype)
        z_ref[2, 1, :, sl] = ys[bm:2 * bm].astype(z_ref.dtype)
        z_ref[1, 0, :, sl] = (ac - b_s).astype(z_ref.dtype)
        z_ref[1, 1, :, sl] = (a_s + bc).astype(z_ref.dtype)
        z_ref[3, 0, :, sl] = (ac + b_s).astype(z_ref.dtype)
        z_ref[3, 1, :, sl] = (a_s - bc).astype(z_ref.dtype)


def _largest_tile(n, cap):
    tile = 16
    while tile * 2 <= cap and n % (tile * 2) == 0:
        tile *= 2
    assert n % tile == 0
    return tile


def _chan_dft(x, g, cs):
    b, s, d = x.shape
    quarter = s // DFT_RADIX
    bm = _largest_tile(quarter, 128)
    tiles = quarter // bm
    x_specs = [pl.BlockSpec((None, bm, d), functools.partial(lambda bi, i, q: (bi, q * tiles + i, 0), q=q))
               for q in range(DFT_RADIX)]
    return pl.pallas_call(
        _chan_dft_kernel,
        grid=(b, tiles),
        in_specs=x_specs + [
            pl.BlockSpec((1, d), lambda bi, i: (0, 0)),
            pl.BlockSpec((FOURIER_GROUP, 2 * FOURIER_GROUP), lambda bi, i: (0, 0)),
        ],
        out_specs=pl.BlockSpec((None, DFT_RADIX, 2, bm, d), lambda bi, i: (bi, 0, 0, i, 0)),
        out_shape=jax.ShapeDtypeStruct((b, DFT_RADIX, 2, quarter, d), BF16),
        compiler_params=_params(("parallel", "parallel")),
        name="chan_dft",
    )(x, x, x, x, g.reshape(1, d), cs)


def _seq_dft_kernel(f_ref, z_ref, o_ref):
    o_ref[...] = jnp.dot(f_ref[...], z_ref[...], preferred_element_type=F32).astype(o_ref.dtype)


def _seq_dft(f, z):
    b, _, half, d = z.shape
    quarter = half // 2
    bm, bn = _largest_tile(quarter, 1024), _largest_tile(d, 1024)
    if quarter <= 1024 and quarter % 16 == 0:
        bm = quarter
    nj = d // bn
    return pl.pallas_call(
        _seq_dft_kernel,
        grid=(b, DFT_RADIX, quarter // bm, nj),
        in_specs=[
            pl.BlockSpec((None, bm, half), lambda bi, r, i, j: (r, i, 0)),
            pl.BlockSpec((None, None, half, bn), lambda bi, r, i, j: (bi, r, 0, j)),
        ],
        out_specs=pl.BlockSpec((None, bm, bn), lambda bi, r, i, j: (bi, i, r * nj + j)),
        out_shape=jax.ShapeDtypeStruct((b, quarter, DFT_RADIX * d), BF16),
        compiler_params=_params(("parallel", "parallel", "parallel", "parallel")),
        name="seq_dft",
    )(f, z)


def _dft_tables(s):
    def angles(freq, n, period):
        return ((freq * n) % period).astype(F32) * (2.0 * jnp.pi / period)

    iota = lambda shape, axis: lax.broadcasted_iota(jnp.int32, shape, axis)
    grp = (FOURIER_GROUP, FOURIER_GROUP)
    ang = angles(iota(grp, 0), iota(grp, 1), FOURIER_GROUP)
    cs = jnp.concatenate([jnp.cos(ang), jnp.sin(ang)], axis=1) * FOURIER_GROUP ** -0.5
    quarter = s // DFT_RADIX
    shape = (DFT_RADIX, quarter, quarter)
    ang = angles(DFT_RADIX * iota(shape, 1) + iota(shape, 0), iota(shape, 2), s)
    seq = jnp.concatenate([jnp.cos(ang), -jnp.sin(ang)], axis=2) * s ** -0.5
    return cs.astype(BF16), seq.astype(BF16)


def _attn_kernel(q_ref, k_ref, v_ref, bias_ref, o_ref, *, n_groups, heads, groups_per_step):
    gq = ATTN_Q_ROWS * GRID_W
    gk = ATTN_K_TILES * gq

    def body(it, carry):
        groups = [it * groups_per_step + u for u in range(groups_per_step)]
        work = [(g, h) for g in groups for h in range(heads)]
        scores = []
        for g, h in work:
            lanes = slice(h * HEAD_DIM, (h + 1) * HEAD_DIM)
            k_start = pl.multiple_of(jnp.clip(g - 1, 0, n_groups - ATTN_K_TILES) * gq, gq)
            q = q_ref[pl.ds(pl.multiple_of(g * gq, gq), gq), lanes]
            k = k_ref[pl.ds(k_start, gk), lanes]
            kind = jnp.where(g == 0, 0, jnp.where(g == n_groups - 1, 2, 1))
            s = lax.dot_general(q, k, (((1,), (1,)), ((), ())), preferred_element_type=F32)
            scores.append(s + bias_ref[h, kind])
        probs, denoms = [], []
        for s in scores:
            p = jnp.exp(s - jnp.max(s, axis=-1, keepdims=True))
            denoms.append(jnp.sum(p, axis=-1, keepdims=True))
            probs.append(p.astype(BF16))
        for (g, h), p, denom in zip(work, probs, denoms):
            lanes = slice(h * HEAD_DIM, (h + 1) * HEAD_DIM)
            k_start = pl.multiple_of(jnp.clip(g - 1, 0, n_groups - ATTN_K_TILES) * gq, gq)
            o = jnp.dot(p, v_ref[pl.ds(k_start, gk), lanes], preferred_element_type=F32)
            o_ref[pl.ds(pl.multiple_of(g * gq, gq), gq), lanes] = (o / denom).astype(o_ref.dtype)
        return carry

    lax.fori_loop(0, n_groups // groups_per_step, body, 0)


def _attention(qkv, bias, layer):
    b, s, d3 = qkv.shape
    d = d3 // 3
    heads = ATTN_HEADS_PER_STEP
    n_head_blocks = d // (heads * HEAD_DIM)
    n_groups = s // (ATTN_Q_ROWS * GRID_W)
    assert s % (ATTN_Q_ROWS * GRID_W) == 0 and n_groups >= ATTN_K_TILES
    groups_per_step = math.gcd(n_groups, ATTN_GROUPS_PER_STEP)
    width = heads * HEAD_DIM
    return pl.pallas_call(
        functools.partial(_attn_kernel, n_groups=n_groups, heads=heads, groups_per_step=groups_per_step),
        grid=(n_head_blocks, b),
        in_specs=[
            pl.BlockSpec((None, s, width), lambda h, bi: (bi, 0, h)),
            pl.BlockSpec((None, s, width), lambda h, bi: (bi, 0, n_head_blocks + h)),
            pl.BlockSpec((None, s, width), lambda h, bi: (bi, 0, 2 * n_head_blocks + h)),
            pl.BlockSpec((None, heads) + bias.shape[2:], lambda h, bi: (layer, h, 0, 0, 0)),
        ],
        out_specs=pl.BlockSpec((None, s, width), lambda h, bi: (bi, 0, h)),
        out_shape=jax.ShapeDtypeStruct((b, s, d), BF16),
        compiler_params=_params(("parallel", "parallel")),
        name="nbr_attention",
    )(qkv, qkv, qkv, bias)


def _bias_table(rpb):
    n_key_rows = ATTN_K_TILES * ATTN_Q_ROWS
    t = np.arange(ATTN_Q_ROWS)[:, None, None, None]
    c = np.arange(GRID_W)[None, :, None, None]
    j = np.arange(n_key_rows)[None, None, :, None]
    kc = np.arange(GRID_W)[None, None, None, :]
    col_start = np.clip(c - WIN_COLS // 2, 0, GRID_W - WIN_COLS)
    col_ok = (kc >= col_start) & (kc < col_start + WIN_COLS)
    col_off = np.clip(kc - c + WIN_COLS - 1, 0, 2 * WIN_COLS - 2)
    row_idx, col_idx, ok = [], [], []
    for q_row, win_row in ((t, 0 * t), (t + ATTN_Q_ROWS, t), (t + 2 * ATTN_Q_ROWS, ATTN_Q_ROWS + 0 * t)):
        row_ok = (j >= win_row) & (j < win_row + WIN_ROWS)
        shape = np.broadcast_shapes(row_ok.shape, col_ok.shape)
        ok.append(np.broadcast_to(row_ok & col_ok, shape))
        row_idx.append(np.broadcast_to(np.clip(j - q_row + WIN_ROWS - 1, 0, 2 * WIN_ROWS - 2), shape))
        col_idx.append(np.broadcast_to(col_off, shape))
    flat = (3, ATTN_Q_ROWS * GRID_W, n_key_rows * GRID_W)
    row_idx, col_idx, ok = (np.stack(a).reshape(flat) for a in (row_idx, col_idx, ok))
    return jnp.where(ok[None], rpb[:, row_idx, col_idx], NEG_INF)


def _mlp_kernel(x_ref, g_ref, win_ref, wout_ref, o_ref, h_ref, u_ref, *, nf):
    s = pl.program_id(1)

    def up(slot):
        u = jnp.maximum(jnp.dot(h_ref[...], win_ref[...], preferred_element_type=F32), 0.0)
        u_ref[slot] = (u * u).astype(BF16)

    def down(slot):
        o_ref[...] += jnp.dot(u_ref[slot], wout_ref[...], preferred_element_type=F32)

    @pl.when(s == 0)
    def _():
        x = x_ref[...]
        h_ref[...] = (x * _rms_scale(x) * g_ref[...]).astype(BF16)
        o_ref[...] = x
        up(0)

    for parity in (0, 1):
        @pl.when(jnp.logical_and(jnp.logical_and(s > 0, s < nf), s % 2 == parity))
        def _():
            down(1 - parity)
            up(parity)

    @pl.when(s == nf)
    def _():
        down((nf - 1) % 2)


def _mlp(x, g, w_in, w_out, layer, *, bm=512, bf=512):
    t, d = x.shape
    f = w_in.shape[2]
    bm, bf = min(bm, t), min(bf, f)
    nf = f // bf
    return pl.pallas_call(
        functools.partial(_mlp_kernel, nf=nf),
        grid=(t // bm, nf + 1),
        in_specs=[
            pl.BlockSpec((bm, d), lambda i, s: (i, 0), pipeline_mode=pl.Buffered(1)),
            pl.BlockSpec((1, d), lambda i, s: (0, 0)),
            pl.BlockSpec((None, d, bf), lambda i, s: (layer, 0, jnp.minimum(s, nf - 1))),
            pl.BlockSpec((None, bf, d), lambda i, s: (layer, jnp.maximum(s - 1, 0), 0)),
        ],
        out_specs=pl.BlockSpec((bm, d), lambda i, s: (i, 0)),
        out_shape=jax.ShapeDtypeStruct((t, d), F32),
        scratch_shapes=[pltpu.VMEM((bm, d), BF16), pltpu.VMEM((2, bm, bf), BF16)],
        compiler_params=_params(("parallel", "arbitrary")),
        name="relu2_mlp",
    )(x, g.reshape(1, d), w_in, w_out)


def _trunk(x, w):
    b, s, d = x.shape
    t = b * s
    n_heads = d // HEAD_DIM
    depth = w["norm_mix"].shape[0]
    cs, seq = _dft_tables(s)
    for layer in range(depth):
        j = layer // N_MIXERS
        if layer % N_MIXERS == 0:
            z = _chan_dft(x, w["norm_mix"][layer], cs)
            mix_in, w_o = _seq_dft(seq, z.reshape(b, DFT_RADIX, s // 2, d)), w["fourier_w_o"]
        else:
            h = _rmsnorm(x.reshape(t, d), w["norm_mix"][layer])
            bn_qkv = min(1024, d)
            qkv = _matmul(h[None], w["na_w_qkv"], bm=1024, bn=bn_qkv, bk=d, out_dtype=BF16,
                          epilogue="headnorm", extra=w["na_qk_gain"], n_norm_tiles=2 * d // bn_qkv,
                          w_layer=j, extra_layer=j, name="qkv_proj")
            mix_in, w_o = _attention(qkv.reshape(b, s, 3 * d), w["na_bias"], j), w["na_w_o"]
        x = _matmul(mix_in.reshape(1, t, d), w_o, bm=1024, bn=512, bk=d, out_dtype=F32,
                    epilogue="residual", extra=x.reshape(1, t, d), w_layer=j, name="out_proj")
        x = _mlp(x.reshape(t, d), w["norm_ffn"][layer], w["ffn_w_in"], w["ffn_w_out"], layer).reshape(b, s, d)
    return x


def kernel(x_prompt, x_sample, norm_mix, norm_ffn, fourier_w_o, na_w_qkv, na_g_q, na_g_k, na_rpb, na_w_o,
           ffn_w_in, ffn_w_out):
    d = x_prompt.shape[-1]
    n_heads = d // HEAD_DIM
    n_na = na_w_qkv.shape[0]
    qk_gain = jnp.concatenate(
        [jnp.tile(na_g_q * HEAD_DIM ** -0.5, (1, n_heads)), jnp.tile(na_g_k, (1, n_heads)),
         jnp.ones((n_na, d), F32)], axis=1).reshape(n_na, 1, 3 * d)
    w = {
        "norm_mix": norm_mix,
        "norm_ffn": norm_ffn,
        "fourier_w_o": fourier_w_o.astype(BF16),
        "na_w_qkv": na_w_qkv.astype(BF16),
        "na_qk_gain": qk_gain,
        "na_bias": jax.vmap(_bias_table)(na_rpb),
        "na_w_o": na_w_o.astype(BF16),
        "ffn_w_in": ffn_w_in.astype(BF16),
        "ffn_w_out": ffn_w_out.astype(BF16),
    }
    return _trunk(x_prompt, w), _trunk(x_sample, w)
```

```python
import functools
import math

import numpy as np
import jax
import jax.numpy as jnp
from jax import lax
from jax.experimental import pallas as pl
from jax.experimental.pallas import tpu as pltpu

GRID_W = 64
HEAD_DIM = 128
FOURIER_GROUP = 128
WIN_ROWS = 8
WIN_COLS = 16
N_MIXERS = 2
DFT_RADIX = 4
EPS = 1e-6
NEG_INF = -1e30

V7X_VMEM_BYTES = 64 * 1024 * 1024
VMEM_LIMIT_BYTES = V7X_VMEM_BYTES - 8 * 1024 * 1024
LANES = 128
ATTN_Q_ROWS = 4
ATTN_K_TILES = 3
ATTN_GROUPS_PER_STEP = 2
ATTN_HEADS_PER_STEP = 2

F32 = jnp.float32
BF16 = jnp.bfloat16


def _params(semantics):
    return pltpu.CompilerParams(dimension_semantics=semantics, vmem_limit_bytes=VMEM_LIMIT_BYTES)


def _rms_scale(x):
    return lax.rsqrt(jnp.mean(x * x, axis=-1, keepdims=True) + EPS)


def _rmsnorm_kernel(x_ref, g_ref, o_ref):
    x = x_ref[...]
    o_ref[...] = (x * _rms_scale(x) * g_ref[...]).astype(o_ref.dtype)


def _rmsnorm(x, g, *, bm=256):
    t, d = x.shape
    bm = min(bm, t)
    return pl.pallas_call(
        _rmsnorm_kernel,
        grid=(t // bm,),
        in_specs=[pl.BlockSpec((bm, d), lambda i: (i, 0)), pl.BlockSpec((1, d), lambda i: (0, 0))],
        out_specs=pl.BlockSpec((bm, d), lambda i: (i, 0)),
        out_shape=jax.ShapeDtypeStruct((t, d), BF16),
        compiler_params=_params(("parallel",)),
        name="rmsnorm",
    )(x, g.reshape(1, d))


def _mm_kernel(*refs, nk, epilogue, n_norm_tiles):
    a_ref, w_ref = refs[0], refs[1]
    extra_ref = refs[2] if epilogue in ("residual", "headnorm") else None
    o_ref = refs[3] if extra_ref is not None else refs[2]
    acc_ref = refs[-1] if nk > 1 else None
    j = pl.program_id(2)
    k = pl.program_id(3)

    def finish(acc):
        if epilogue == "cast":
            o_ref[...] = acc.astype(o_ref.dtype)
        elif epilogue == "residual":
            o_ref[...] = (extra_ref[...] + acc).astype(o_ref.dtype)
        else:
            @pl.when(j < n_norm_tiles)
            def _():
                for h in range(acc.shape[1] // HEAD_DIM):
                    sl = slice(h * HEAD_DIM, (h + 1) * HEAD_DIM)
                    a = acc[:, sl]
                    o_ref[:, sl] = (a * _rms_scale(a) * extra_ref[:, sl]).astype(o_ref.dtype)

            @pl.when(j >= n_norm_tiles)
            def _():
                o_ref[...] = acc.astype(o_ref.dtype)

    part = jnp.dot(a_ref[...], w_ref[...], preferred_element_type=F32)
    if nk == 1:
        finish(part)
    else:
        @pl.when(k == 0)
        def _():
            acc_ref[...] = part

        @pl.when(jnp.logical_and(k > 0, k < nk - 1))
        def _():
            acc_ref[...] += part

        @pl.when(k == nk - 1)
        def _():
            finish(acc_ref[...] + part)


def _matmul(a, w, *, bm, bn, bk, out_dtype, epilogue="cast", extra=None, n_norm_tiles=0, w_layer=None,
            extra_layer=None, name):
    ba, m, kdim = a.shape
    bw, _, n = w.shape
    if w_layer is not None:
        bw = 1
    nb = max(ba, bw)
    bm, bn, bk = min(bm, m), min(bn, n), min(bk, kdim)
    nk = kdim // bk
    a_map = (lambda b, i, j, k: (b, i, k)) if ba > 1 else (lambda b, i, j, k: (0, i, k))
    w_first = 0 if w_layer is None else w_layer
    w_map = (lambda b, i, j, k: (b, k, j)) if bw > 1 else (lambda b, i, j, k: (w_first, k, j))
    in_specs = [pl.BlockSpec((None, bm, bk), a_map), pl.BlockSpec((None, bk, bn), w_map)]
    args = [a, w]
    if epilogue == "residual":
        in_specs.append(pl.BlockSpec((None, bm, bn), lambda b, i, j, k: (b, i, j)))
        args.append(extra)
    elif epilogue == "headnorm":
        in_specs.append(pl.BlockSpec((None, 1, bn), lambda b, i, j, k: (extra_layer, 0, j)))
        args.append(extra)
    return pl.pallas_call(
        functools.partial(_mm_kernel, nk=nk, epilogue=epilogue, n_norm_tiles=n_norm_tiles),
        grid=(nb, m // bm, n // bn, nk),
        in_specs=in_specs,
        out_specs=pl.BlockSpec((None, bm, bn), lambda b, i, j, k: (b, i, j)),
        out_shape=jax.ShapeDtypeStruct((nb, m, n), out_dtype),
        scratch_shapes=[pltpu.VMEM((bm, bn), F32)] if nk > 1 else [],
        compiler_params=_params(("parallel", "parallel", "parallel", "arbitrary")),
        name=name,
    )(*args)


def _chan_dft_kernel(x0_ref, x1_ref, x2_ref, x3_ref, g_ref, cs_ref, z_ref):
    h0, h1, h2, h3 = ((x_ref[...] * _rms_scale(x_ref[...]) * g_ref[...]) for x_ref in (x0_ref, x1_ref, x2_ref, x3_ref))
    even, odd = h0 + h2, h1 + h3
    lhs = jnp.concatenate([even + odd, even - odd, h0 - h2, h1 - h3], axis=0).astype(BF16)
    bm = h0.shape[0]
    for grp in range(h0.shape[1] // FOURIER_GROUP):
        sl = slice(grp * FOURIER_GROUP, (grp + 1) * FOURIER_GROUP)
        y = jnp.dot(lhs[:, sl], cs_ref[...], preferred_element_type=F32)
        yc, ys = y[:, :FOURIER_GROUP], y[:, FOURIER_GROUP:]
        ac, a_s, bc, b_s = yc[2 * bm:3 * bm], ys[2 * bm:3 * bm], yc[3 * bm:], ys[3 * bm:]
        z_ref[0, 0, :, sl] = yc[:bm].astype(z_ref.dtype)
        z_ref[0, 1, :, sl] = ys[:bm].astype(z_ref.dtype)
        z_ref[2, 0, :, sl] = yc[bm:2 * bm].astype(z_ref.dtype)
        z_ref[2, 1, :, sl] = ys[bm:2 * bm].astype(z_ref.dtype)
        z_ref[1, 0, :, sl] = (ac - b_s).astype(z_ref.dtype)
        z_ref[1, 1, :, sl] = (a_s + bc).astype(z_ref.dtype)
        z_ref[3, 0, :, sl] = (ac + b_s).astype(z_ref.dtype)
        z_ref[3, 1, :, sl] = (a_s - bc).astype(z_ref.dtype)


def _largest_tile(n, cap):
    tile = 16
    while tile * 2 <= cap and n % (tile * 2) == 0:
        tile *= 2
    assert n % tile == 0
    return tile


def _chan_dft(x, g, cs):
    b, s, d = x.shape
    quarter = s // DFT_RADIX
    bm = _largest_tile(quarter, 128)
    tiles = quarter // bm
    x_specs = [pl.BlockSpec((None, bm, d), functools.partial(lambda bi, i, q: (bi, q * tiles + i, 0), q=q))
               for q in range(DFT_RADIX)]
    return pl.pallas_call(
        _chan_dft_kernel,
        grid=(b, tiles),
        in_specs=x_specs + [
            pl.BlockSpec((1, d), lambda bi, i: (0, 0)),
            pl.BlockSpec((FOURIER_GROUP, 2 * FOURIER_GROUP), lambda bi, i: (0, 0)),
        ],
        out_specs=pl.BlockSpec((None, DFT_RADIX, 2, bm, d), lambda bi, i: (bi, 0, 0, i, 0)),
        out_shape=jax.ShapeDtypeStruct((b, DFT_RADIX, 2, quarter, d), BF16),
        compiler_params=_params(("parallel", "parallel")),
        name="chan_dft",
    )(x, x, x, x, g.reshape(1, d), cs)


def _seq_dft_kernel(f_ref, z_ref, o_ref):
    o_ref[...] = jnp.dot(f_ref[...], z_ref[...], preferred_element_type=F32).astype(o_ref.dtype)


def _seq_dft(f, z):
    b, _, half, d = z.shape
    quarter = half // 2
    bm, bn = _largest_tile(quarter, 1024), _largest_tile(d, 1024)
    if quarter <= 1024 and quarter % 16 == 0:
        bm = quarter
    nj = d // bn
    return pl.pallas_call(
        _seq_dft_kernel,
        grid=(b, DFT_RADIX, quarter // bm, nj),
        in_specs=[
            pl.BlockSpec((None, bm, half), lambda bi, r, i, j: (r, i, 0)),
            pl.BlockSpec((None, None, half, bn), lambda bi, r, i, j: (bi, r, 0, j)),
        ],
        out_specs=pl.BlockSpec((None, bm, bn), lambda bi, r, i, j: (bi, i, r * nj + j)),
        out_shape=jax.ShapeDtypeStruct((b, quarter, DFT_RADIX * d), BF16),
        compiler_params=_params(("parallel", "parallel", "parallel", "parallel")),
        name="seq_dft",
    )(f, z)


def _dft_tables(s):
    def angles(freq, n, period):
        return ((freq * n) % period).astype(F32) * (2.0 * jnp.pi / period)

    iota = lambda shape, axis: lax.broadcasted_iota(jnp.int32, shape, axis)
    grp = (FOURIER_GROUP, FOURIER_GROUP)
    ang = angles(iota(grp, 0), iota(grp, 1), FOURIER_GROUP)
    cs = jnp.concatenate([jnp.cos(ang), jnp.sin(ang)], axis=1) * FOURIER_GROUP ** -0.5
    quarter = s // DFT_RADIX
    shape = (DFT_RADIX, quarter, quarter)
    ang = angles(DFT_RADIX * iota(shape, 1) + iota(shape, 0), iota(shape, 2), s)
    seq = jnp.concatenate([jnp.cos(ang), -jnp.sin(ang)], axis=2) * s ** -0.5
    return cs.astype(BF16), seq.astype(BF16)


def _attn_kernel(q_ref, k_ref, v_ref, bias_ref, o_ref, *, n_groups, heads, groups_per_step):
    gq = ATTN_Q_ROWS * GRID_W
    gk = ATTN_K_TILES * gq

    def body(it, carry):
        groups = [it * groups_per_step + u for u in range(groups_per_step)]
        work = [(g, h) for g in groups for h in range(heads)]
        scores = []
        for g, h in work:
            lanes = slice(h * HEAD_DIM, (h + 1) * HEAD_DIM)
            k_start = pl.multiple_of(jnp.clip(g - 1, 0, n_groups - ATTN_K_TILES) * gq, gq)
            q = q_ref[pl.ds(pl.multiple_of(g * gq, gq), gq), lanes]
            k = k_ref[pl.ds(k_start, gk), lanes]
            kind = jnp.where(g == 0, 0, jnp.where(g == n_groups - 1, 2, 1))
            s = lax.dot_general(q, k, (((1,), (1,)), ((), ())), preferred_element_type=F32)
            scores.append(s + bias_ref[h, kind])
        probs, denoms = [], []
        for s in scores:
            p = jnp.exp(s - jnp.max(s, axis=-1, keepdims=True))
            denoms.append(jnp.sum(p, axis=-1, keepdims=True))
            probs.append(p.astype(BF16))
        for (g, h), p, denom in zip(work, probs, denoms):
            lanes = slice(h * HEAD_DIM, (h + 1) * HEAD_DIM)
            k_start = pl.multiple_of(jnp.clip(g - 1, 0, n_groups - ATTN_K_TILES) * gq, gq)
            o = jnp.dot(p, v_ref[pl.ds(k_start, gk), lanes], preferred_element_type=F32)
            o_ref[pl.ds(pl.multiple_of(g * gq, gq), gq), lanes] = (o / denom).astype(o_ref.dtype)
        return carry

    lax.fori_loop(0, n_groups // groups_per_step, body, 0)


def _attention(qkv, bias, layer):
    b, s, d3 = qkv.shape
    d = d3 // 3
    heads = ATTN_HEADS_PER_STEP
    n_head_blocks = d // (heads * HEAD_DIM)
    n_groups = s // (ATTN_Q_ROWS * GRID_W)
    assert s % (ATTN_Q_ROWS * GRID_W) == 0 and n_groups >= ATTN_K_TILES
    groups_per_step = math.gcd(n_groups, ATTN_GROUPS_PER_STEP)
    width = heads * HEAD_DIM
    return pl.pallas_call(
        functools.partial(_attn_kernel, n_groups=n_groups, heads=heads, groups_per_step=groups_per_step),
        grid=(n_head_blocks, b),
        in_specs=[
            pl.BlockSpec((None, s, width), lambda h, bi: (bi, 0, h)),
            pl.BlockSpec((None, s, width), lambda h, bi: (bi, 0, n_head_blocks + h)),
            pl.BlockSpec((None, s, width), lambda h, bi: (bi, 0, 2 * n_head_blocks + h)),
            pl.BlockSpec((None, heads) + bias.shape[2:], lambda h, bi: (layer, h, 0, 0, 0)),
        ],
        out_specs=pl.BlockSpec((None, s, width), lambda h, bi: (bi, 0, h)),
        out_shape=jax.ShapeDtypeStruct((b, s, d), BF16),
        compiler_params=_params(("parallel", "parallel")),
        name="nbr_attention",
    )(qkv, qkv, qkv, bias)


def _bias_table(rpb):
    n_key_rows = ATTN_K_TILES * ATTN_Q_ROWS
    t = np.arange(ATTN_Q_ROWS)[:, None, None, None]
    c = np.arange(GRID_W)[None, :, None, None]
    j = np.arange(n_key_rows)[None, None, :, None]
    kc = np.arange(GRID_W)[None, None, None, :]
    col_start = np.clip(c - WIN_COLS // 2, 0, GRID_W - WIN_COLS)
    col_ok = (kc >= col_start) & (kc < col_start + WIN_COLS)
    n_heads, n_rel_rows, n_rel_cols = rpb.shape
    period = 2 * GRID_W
    left = GRID_W - WIN_COLS
    u = jnp.pad(rpb, ((0, 0), (0, 0), (left, period - n_rel_cols - left)))
    tiled = jnp.broadcast_to(u[:, :, None, :], (n_heads, n_rel_rows, GRID_W, period))
    tiled = tiled.reshape(n_heads, n_rel_rows, GRID_W * period)[:, :, :GRID_W * (period - 1)]
    toep = tiled.reshape(n_heads, n_rel_rows, GRID_W, period - 1)[..., GRID_W - 1:]
    pad = n_key_rows - WIN_ROWS
    toep = jnp.pad(toep, ((0, 0), (pad, pad), (0, 0), (0, 0)))
    blocks, ok = [], []
    for first_q_row, win_row in ((0, 0 * t), (ATTN_Q_ROWS, t), (2 * ATTN_Q_ROWS, ATTN_Q_ROWS + 0 * t)):
        ok.append(np.broadcast_to((j >= win_row) & (j < win_row + WIN_ROWS) & col_ok,
                                  (ATTN_Q_ROWS, GRID_W, n_key_rows, GRID_W)))
        for tq in range(ATTN_Q_ROWS):
            rho0 = WIN_ROWS - 1 - (first_q_row + tq) + pad
            blocks.append(toep[:, rho0:rho0 + n_key_rows])
    table = jnp.stack(blocks, axis=1).reshape(n_heads, 3, ATTN_Q_ROWS, n_key_rows, GRID_W, GRID_W)
    table = table.transpose(0, 1, 2, 4, 3, 5)
    flat = (3, ATTN_Q_ROWS * GRID_W, n_key_rows * GRID_W)
    return jnp.where(np.stack(ok).reshape(flat)[None], table.reshape((n_heads,) + flat), NEG_INF)


def _mlp_kernel(x_ref, g_ref, win_ref, wout_ref, o_ref, h_ref, u_ref, *, nf):
    s = pl.program_id(1)

    def up(slot):
        u = jnp.maximum(jnp.dot(h_ref[...], win_ref[...], preferred_element_type=F32), 0.0)
        u_ref[slot] = (u * u).astype(BF16)

    def down(slot):
        o_ref[...] += jnp.dot(u_ref[slot], wout_ref[...], preferred_element_type=F32)

    @pl.when(s == 0)
    def _():
        x = x_ref[...]
        h_ref[...] = (x * _rms_scale(x) * g_ref[...]).astype(BF16)
        o_ref[...] = x
        up(0)

    for parity in (0, 1):
        @pl.when(jnp.logical_and(jnp.logical_and(s > 0, s < nf), s % 2 == parity))
        def _():
            down(1 - parity)
            up(parity)

    @pl.when(s == nf)
    def _():
        down((nf - 1) % 2)


def _mlp(x, g, w_in, w_out, layer, *, bm=512, bf=512):
    t, d = x.shape
    f = w_in.shape[2]
    bm, bf = min(bm, t), min(bf, f)
    nf = f // bf
    return pl.pallas_call(
        functools.partial(_mlp_kernel, nf=nf),
        grid=(t // bm, nf + 1),
        in_specs=[
            pl.BlockSpec((bm, d), lambda i, s: (i, 0), pipeline_mode=pl.Buffered(1)),
            pl.BlockSpec((1, d), lambda i, s: (0, 0)),
            pl.BlockSpec((None, d, bf), lambda i, s: (layer, 0, jnp.minimum(s, nf - 1))),
            pl.BlockSpec((None, bf, d), lambda i, s: (layer, jnp.maximum(s - 1, 0), 0)),
        ],
        out_specs=pl.BlockSpec((bm, d), lambda i, s: (i, 0)),
        out_shape=jax.ShapeDtypeStruct((t, d), F32),
        scratch_shapes=[pltpu.VMEM((bm, d), BF16), pltpu.VMEM((2, bm, bf), BF16)],
        compiler_params=_params(("parallel", "arbitrary")),
        name="relu2_mlp",
    )(x, g.reshape(1, d), w_in, w_out)


def _trunk(x, w):
    b, s, d = x.shape
    t = b * s
    n_heads = d // HEAD_DIM
    depth = w["norm_mix"].shape[0]
    cs, seq = _dft_tables(s)
    for layer in range(depth):
        j = layer // N_MIXERS
        if layer % N_MIXERS == 0:
            z = _chan_dft(x, w["norm_mix"][layer], cs)
            mix_in, w_o = _seq_dft(seq, z.reshape(b, DFT_RADIX, s // 2, d)), w["fourier_w_o"]
        else:
            h = _rmsnorm(x.reshape(t, d), w["norm_mix"][layer])
            bn_qkv = min(1024, d)
            qkv = _matmul(h[None], w["na_w_qkv"], bm=1024, bn=bn_qkv, bk=d, out_dtype=BF16,
                          epilogue="headnorm", extra=w["na_qk_gain"], n_norm_tiles=2 * d // bn_qkv,
                          w_layer=j, extra_layer=j, name="qkv_proj")
            mix_in, w_o = _attention(qkv.reshape(b, s, 3 * d), w["na_bias"], j), w["na_w_o"]
        x = _matmul(mix_in.reshape(1, t, d), w_o, bm=1024, bn=512, bk=d, out_dtype=F32,
                    epilogue="residual", extra=x.reshape(1, t, d), w_layer=j, name="out_proj")
        x = _mlp(x.reshape(t, d), w["norm_ffn"][layer], w["ffn_w_in"], w["ffn_w_out"], layer).reshape(b, s, d)
    return x


def kernel(x_prompt, x_sample, norm_mix, norm_ffn, fourier_w_o, na_w_qkv, na_g_q, na_g_k, na_rpb, na_w_o,
           ffn_w_in, ffn_w_out):
    d = x_prompt.shape[-1]
    n_heads = d // HEAD_DIM
    n_na = na_w_qkv.shape[0]
    qk_gain = jnp.concatenate(
        [jnp.tile(na_g_q * HEAD_DIM ** -0.5, (1, n_heads)), jnp.tile(na_g_k, (1, n_heads)),
         jnp.ones((n_na, d), F32)], axis=1).reshape(n_na, 1, 3 * d)
    w = {
        "norm_mix": norm_mix,
        "norm_ffn": norm_ffn,
        "fourier_w_o": fourier_w_o.astype(BF16),
        "na_w_qkv": na_w_qkv.astype(BF16),
        "na_qk_gain": qk_gain,
        "na_bias": jax.vmap(_bias_table)(na_rpb),
        "na_w_o": na_w_o.astype(BF16),
        "ffn_w_in": ffn_w_in.astype(BF16),
        "ffn_w_out": ffn_w_out.astype(BF16),
    }
    return _trunk(x_prompt, w), _trunk(x_sample, w)
```

```python
import functools
import math

import numpy as np
import jax
import jax.numpy as jnp
from jax import lax
from jax.experimental import pallas as pl
from jax.experimental.pallas import tpu as pltpu

GRID_W = 64
HEAD_DIM = 128
FOURIER_GROUP = 128
WIN_ROWS = 8
WIN_COLS = 16
N_MIXERS = 2
DFT_RADIX = 4
EPS = 1e-6
NEG_INF = -1e30
LOG2_E = math.log2(math.e)

V7X_VMEM_BYTES = 64 * 1024 * 1024
VMEM_LIMIT_BYTES = V7X_VMEM_BYTES - 8 * 1024 * 1024
LANES = 128
ATTN_Q_ROWS = 4
ATTN_K_TILES = 3
ATTN_GROUPS_PER_STEP = 2
ATTN_HEADS_PER_STEP = 2

F32 = jnp.float32
BF16 = jnp.bfloat16


def _params(semantics):
    return pltpu.CompilerParams(dimension_semantics=semantics, vmem_limit_bytes=VMEM_LIMIT_BYTES)


def _rms_scale(x):
    return lax.rsqrt(jnp.mean(x * x, axis=-1, keepdims=True) + EPS)


def _rmsnorm_kernel(x_ref, g_ref, o_ref):
    x = x_ref[...]
    o_ref[...] = (x * _rms_scale(x) * g_ref[...]).astype(o_ref.dtype)


def _rmsnorm(x, g, *, bm=256):
    t, d = x.shape
    bm = min(bm, t)
    return pl.pallas_call(
        _rmsnorm_kernel,
        grid=(t // bm,),
        in_specs=[pl.BlockSpec((bm, d), lambda i: (i, 0)), pl.BlockSpec((1, d), lambda i: (0, 0))],
        out_specs=pl.BlockSpec((bm, d), lambda i: (i, 0)),
        out_shape=jax.ShapeDtypeStruct((t, d), BF16),
        compiler_params=_params(("parallel",)),
        name="rmsnorm",
    )(x, g.reshape(1, d))


def _mm_kernel(*refs, nk, epilogue, n_norm_tiles):
    a_ref, w_ref = refs[0], refs[1]
    extra_ref = refs[2] if epilogue in ("residual", "headnorm") else None
    o_ref = refs[3] if extra_ref is not None else refs[2]
    acc_ref = refs[-1] if nk > 1 else None
    j = pl.program_id(2)
    k = pl.program_id(3)

    def finish(acc):
        if epilogue == "cast":
            o_ref[...] = acc.astype(o_ref.dtype)
        elif epilogue == "residual":
            o_ref[...] = (extra_ref[...] + acc).astype(o_ref.dtype)
        else:
            @pl.when(j < n_norm_tiles)
            def _():
                for h in range(acc.shape[1] // HEAD_DIM):
                    sl = slice(h * HEAD_DIM, (h + 1) * HEAD_DIM)
                    a = acc[:, sl]
                    o_ref[:, sl] = (a * _rms_scale(a) * extra_ref[:, sl]).astype(o_ref.dtype)

            @pl.when(j >= n_norm_tiles)
            def _():
                o_ref[...] = acc.astype(o_ref.dtype)

    part = jnp.dot(a_ref[...], w_ref[...].astype(a_ref.dtype), preferred_element_type=F32)
    if nk == 1:
        finish(part)
    else:
        @pl.when(k == 0)
        def _():
            acc_ref[...] = part

        @pl.when(jnp.logical_and(k > 0, k < nk - 1))
        def _():
            acc_ref[...] += part

        @pl.when(k == nk - 1)
        def _():
            finish(acc_ref[...] + part)


def _matmul(a, w, *, bm, bn, bk, out_dtype, epilogue="cast", extra=None, n_norm_tiles=0, w_layer=None,
            extra_layer=None, name):
    ba, m, kdim = a.shape
    bw, _, n = w.shape
    if w_layer is not None:
        bw = 1
    nb = max(ba, bw)
    bm, bn, bk = min(bm, m), min(bn, n), min(bk, kdim)
    nk = kdim // bk
    a_map = (lambda b, i, j, k: (b, i, k)) if ba > 1 else (lambda b, i, j, k: (0, i, k))
    w_first = 0 if w_layer is None else w_layer
    w_map = (lambda b, i, j, k: (b, k, j)) if bw > 1 else (lambda b, i, j, k: (w_first, k, j))
    in_specs = [pl.BlockSpec((None, bm, bk), a_map), pl.BlockSpec((None, bk, bn), w_map)]
    args = [a, w]
    if epilogue == "residual":
        in_specs.append(pl.BlockSpec((None, bm, bn), lambda b, i, j, k: (b, i, j)))
        args.append(extra)
    elif epilogue == "headnorm":
        in_specs.append(pl.BlockSpec((None, 1, bn), lambda b, i, j, k: (extra_layer, 0, j)))
        args.append(extra)
    return pl.pallas_call(
        functools.partial(_mm_kernel, nk=nk, epilogue=epilogue, n_norm_tiles=n_norm_tiles),
        grid=(nb, m // bm, n // bn, nk),
        in_specs=in_specs,
        out_specs=pl.BlockSpec((None, bm, bn), lambda b, i, j, k: (b, i, j)),
        out_shape=jax.ShapeDtypeStruct((nb, m, n), out_dtype),
        scratch_shapes=[pltpu.VMEM((bm, bn), F32)] if nk > 1 else [],
        compiler_params=_params(("parallel", "parallel", "parallel", "arbitrary")),
        name=name,
    )(*args)


def _chan_dft_kernel(x0_ref, x1_ref, x2_ref, x3_ref, g_ref, cs_ref, z_ref):
    h0, h1, h2, h3 = ((x_ref[...] * _rms_scale(x_ref[...]) * g_ref[...]) for x_ref in (x0_ref, x1_ref, x2_ref, x3_ref))
    even, odd = h0 + h2, h1 + h3
    lhs = jnp.concatenate([even + odd, even - odd, h0 - h2, h1 - h3], axis=0).astype(BF16)
    bm = h0.shape[0]
    for grp in range(h0.shape[1] // FOURIER_GROUP):
        sl = slice(grp * FOURIER_GROUP, (grp + 1) * FOURIER_GROUP)
        y = jnp.dot(lhs[:, sl], cs_ref[...], preferred_element_type=F32)
        yc, ys = y[:, :FOURIER_GROUP], y[:, FOURIER_GROUP:]
        ac, a_s, bc, b_s = yc[2 * bm:3 * bm], ys[2 * bm:3 * bm], yc[3 * bm:], ys[3 * bm:]
        z_ref[0, 0, :, sl] = yc[:bm].astype(z_ref.dtype)
        z_ref[0, 1, :, sl] = ys[:bm].astype(z_ref.dtype)
        z_ref[2, 0, :, sl] = yc[bm:2 * bm].astype(z_ref.dtype)
        z_ref[2, 1, :, sl] = ys[bm:2 * bm].astype(z_ref.dtype)
        z_ref[1, 0, :, sl] = (ac - b_s).astype(z_ref.dtype)
        z_ref[1, 1, :, sl] = (a_s + bc).astype(z_ref.dtype)
        z_ref[3, 0, :, sl] = (ac + b_s).astype(z_ref.dtype)
        z_ref[3, 1, :, sl] = (a_s - bc).astype(z_ref.dtype)


def _largest_tile(n, cap):
    tile = 16
    while tile * 2 <= cap and n % (tile * 2) == 0:
        tile *= 2
    assert n % tile == 0
    return tile


def _chan_dft(x, g, cs):
    b, s, d = x.shape
    quarter = s // DFT_RADIX
    bm = _largest_tile(quarter, 128)
    tiles = quarter // bm
    x_specs = [pl.BlockSpec((None, bm, d), functools.partial(lambda bi, i, q: (bi, q * tiles + i, 0), q=q))
               for q in range(DFT_RADIX)]
    return pl.pallas_call(
        _chan_dft_kernel,
        grid=(b, tiles),
        in_specs=x_specs + [
            pl.BlockSpec((1, d), lambda bi, i: (0, 0)),
            pl.BlockSpec((FOURIER_GROUP, 2 * FOURIER_GROUP), lambda bi, i: (0, 0)),
        ],
        out_specs=pl.BlockSpec((None, DFT_RADIX, 2, bm, d), lambda bi, i: (bi, 0, 0, i, 0)),
        out_shape=jax.ShapeDtypeStruct((b, DFT_RADIX, 2, quarter, d), BF16),
        compiler_params=_params(("parallel", "parallel")),
        name="chan_dft",
    )(x, x, x, x, g.reshape(1, d), cs)


def _seq_dft_kernel(f_ref, z_ref, o_ref):
    o_ref[...] = jnp.dot(f_ref[...], z_ref[...], preferred_element_type=F32).astype(o_ref.dtype)


def _seq_dft(f, z):
    b, _, half, d = z.shape
    quarter = half // 2
    bm, bn = _largest_tile(quarter, 1024), _largest_tile(d, 1024)
    if quarter <= 1024 and quarter % 16 == 0:
        bm = quarter
    nj = d // bn
    return pl.pallas_call(
        _seq_dft_kernel,
        grid=(b, DFT_RADIX, quarter // bm, nj),
        in_specs=[
            pl.BlockSpec((None, bm, half), lambda bi, r, i, j: (r, i, 0)),
            pl.BlockSpec((None, None, half, bn), lambda bi, r, i, j: (bi, r, 0, j)),
        ],
        out_specs=pl.BlockSpec((None, bm, bn), lambda bi, r, i, j: (bi, i, r * nj + j)),
        out_shape=jax.ShapeDtypeStruct((b, quarter, DFT_RADIX * d), BF16),
        compiler_params=_params(("parallel", "parallel", "parallel", "parallel")),
        name="seq_dft",
    )(f, z)


def _dft_tables(s):
    def angles(freq, n, period):
        return ((freq * n) % period).astype(F32) * (2.0 * jnp.pi / period)

    iota = lambda shape, axis: lax.broadcasted_iota(jnp.int32, shape, axis)
    grp = (FOURIER_GROUP, FOURIER_GROUP)
    ang = angles(iota(grp, 0), iota(grp, 1), FOURIER_GROUP)
    cs = jnp.concatenate([jnp.cos(ang), jnp.sin(ang)], axis=1) * FOURIER_GROUP ** -0.5
    quarter = s // DFT_RADIX
    shape = (DFT_RADIX, quarter, quarter)
    ang = angles(DFT_RADIX * iota(shape, 1) + iota(shape, 0), iota(shape, 2), s)
    seq = jnp.concatenate([jnp.cos(ang), -jnp.sin(ang)], axis=2) * s ** -0.5
    return cs.astype(BF16), seq.astype(BF16)


def _attn_kernel(q_ref, k_ref, v_ref, bias_ref, o_ref, *, n_groups, heads, groups_per_step):
    gq = ATTN_Q_ROWS * GRID_W
    gk = ATTN_K_TILES * gq

    def body(it, carry):
        groups = [it * groups_per_step + u for u in range(groups_per_step)]
        work = [(g, h) for g in groups for h in range(heads)]
        scores = []
        for g, h in work:
            lanes = slice(h * HEAD_DIM, (h + 1) * HEAD_DIM)
            k_start = pl.multiple_of(jnp.clip(g - 1, 0, n_groups - ATTN_K_TILES) * gq, gq)
            q = q_ref[pl.ds(pl.multiple_of(g * gq, gq), gq), lanes]
            k = k_ref[pl.ds(k_start, gk), lanes]
            kind = jnp.where(g == 0, 0, jnp.where(g == n_groups - 1, 2, 1))
            s = lax.dot_general(q, k, (((1,), (1,)), ((), ())), preferred_element_type=F32)
            scores.append(s + bias_ref[h, kind])
        probs, denoms = [], []
        for s in scores:
            p = jnp.exp2(s - jnp.max(s, axis=-1, keepdims=True))
            denoms.append(jnp.sum(p, axis=-1, keepdims=True))
            probs.append(p.astype(BF16))
        for (g, h), p, denom in zip(work, probs, denoms):
            lanes = slice(h * HEAD_DIM, (h + 1) * HEAD_DIM)
            k_start = pl.multiple_of(jnp.clip(g - 1, 0, n_groups - ATTN_K_TILES) * gq, gq)
            o = jnp.dot(p, v_ref[pl.ds(k_start, gk), lanes], preferred_element_type=F32)
            o_ref[pl.ds(pl.multiple_of(g * gq, gq), gq), lanes] = (o / denom).astype(o_ref.dtype)
        return carry

    lax.fori_loop(0, n_groups // groups_per_step, body, 0)


def _attention(qkv, bias, layer):
    b, s, d3 = qkv.shape
    d = d3 // 3
    heads = ATTN_HEADS_PER_STEP
    n_head_blocks = d // (heads * HEAD_DIM)
    n_groups = s // (ATTN_Q_ROWS * GRID_W)
    assert s % (ATTN_Q_ROWS * GRID_W) == 0 and n_groups >= ATTN_K_TILES
    groups_per_step = math.gcd(n_groups, ATTN_GROUPS_PER_STEP)
    width = heads * HEAD_DIM
    return pl.pallas_call(
        functools.partial(_attn_kernel, n_groups=n_groups, heads=heads, groups_per_step=groups_per_step),
        grid=(n_head_blocks, b),
        in_specs=[
            pl.BlockSpec((None, s, width), lambda h, bi: (bi, 0, h)),
            pl.BlockSpec((None, s, width), lambda h, bi: (bi, 0, n_head_blocks + h)),
            pl.BlockSpec((None, s, width), lambda h, bi: (bi, 0, 2 * n_head_blocks + h)),
            pl.BlockSpec((None, heads) + bias.shape[2:], lambda h, bi: (layer, h, 0, 0, 0)),
        ],
        out_specs=pl.BlockSpec((None, s, width), lambda h, bi: (bi, 0, h)),
        out_shape=jax.ShapeDtypeStruct((b, s, d), BF16),
        compiler_params=_params(("parallel", "parallel")),
        name="nbr_attention",
    )(qkv, qkv, qkv, bias)


def _bias_table_kernel(rpb_ref, o_ref):
    n_key_rows = ATTN_K_TILES * ATTN_Q_ROWS
    n_rel_rows = 2 * WIN_ROWS - 1
    shape = (GRID_W, LANES)
    c = lax.broadcasted_iota(jnp.int32, shape, 0)
    lane = lax.broadcasted_iota(jnp.int32, shape, 1)
    low = lane < GRID_W
    kc = lane & (GRID_W - 1)
    col_start = jnp.clip(c - WIN_COLS // 2, 0, GRID_W - WIN_COLS)
    col_ok = jnp.logical_and(kc >= col_start, kc < col_start + WIN_COLS)
    neg = jnp.full(shape, NEG_INF, F32)

    def toep(rho, extra_shift):
        row = jnp.broadcast_to(rpb_ref[rho:rho + 1, :], shape)
        return pltpu.roll(row, extra_shift, 1, stride=1, stride_axis=0)

    def pair(rho):
        lo = toep(rho, 0) if 0 <= rho < n_rel_rows else neg
        hi = toep(rho + 1, GRID_W) if 0 <= rho + 1 < n_rel_rows else neg
        return jnp.where(low, lo, hi)

    kinds = ((0, lambda t: 0), (ATTN_Q_ROWS, lambda t: t), (2 * ATTN_Q_ROWS, lambda t: ATTN_Q_ROWS))
    for kind, (first_q_row, win_start) in enumerate(kinds):
        for t in range(ATTN_Q_ROWS):
            w0 = win_start(t)
            for jp in range(n_key_rows // 2):
                ok_lo = w0 <= 2 * jp < w0 + WIN_ROWS
                ok_hi = w0 <= 2 * jp + 1 < w0 + WIN_ROWS
                if ok_lo or ok_hi:
                    rows_ok = col_ok if (ok_lo and ok_hi) else jnp.logical_and(col_ok, low if ok_lo else ~low)
                    blk = jnp.where(rows_ok, pair(2 * jp - (first_q_row + t) + WIN_ROWS - 1), neg)
                else:
                    blk = neg
                o_ref[kind, t * GRID_W:(t + 1) * GRID_W, jp * LANES:(jp + 1) * LANES] = blk


def _bias_table(rpb):
    n_layers, n_heads, n_rel_rows, n_rel_cols = rpb.shape
    rows = jnp.pad(rpb, ((0, 0), (0, 0), (0, 16 - n_rel_rows), (0, LANES - n_rel_cols)))
    rows = jnp.roll(rows, -(WIN_COLS - 1), axis=-1)
    table_shape = (3, ATTN_Q_ROWS * GRID_W, ATTN_K_TILES * ATTN_Q_ROWS * GRID_W)
    return pl.pallas_call(
        _bias_table_kernel,
        grid=(n_layers, n_heads),
        in_specs=[pl.BlockSpec((None, None, 16, LANES), lambda l, h: (l, h, 0, 0))],
        out_specs=pl.BlockSpec((None, None) + table_shape, lambda l, h: (l, h, 0, 0, 0)),
        out_shape=jax.ShapeDtypeStruct((n_layers, n_heads) + table_shape, F32),
        compiler_params=_params(("parallel", "parallel")),
        name="bias_table",
    )(rows)


def _mlp_kernel(x_ref, g_ref, win_ref, wout_ref, o_ref, h_ref, u_ref, *, nf):
    s = pl.program_id(1)

    def up(slot):
        u = jnp.maximum(jnp.dot(h_ref[...], win_ref[...], preferred_element_type=F32), 0.0)
        u_ref[slot] = (u * u).astype(BF16)

    def down(slot):
        o_ref[...] += jnp.dot(u_ref[slot], wout_ref[...], preferred_element_type=F32)

    @pl.when(s == 0)
    def _():
        x = x_ref[...]
        h_ref[...] = (x * _rms_scale(x) * g_ref[...]).astype(BF16)
        o_ref[...] = x
        up(0)

    for parity in (0, 1):
        @pl.when(jnp.logical_and(jnp.logical_and(s > 0, s < nf), s % 2 == parity))
        def _():
            down(1 - parity)
            up(parity)

    @pl.when(s == nf)
    def _():
        down((nf - 1) % 2)


def _mlp(x, g, w_in, w_out, layer, *, bm=512, bf=512):
    t, d = x.shape
    f = w_in.shape[2]
    bm, bf = min(bm, t), min(bf, f)
    nf = f // bf
    return pl.pallas_call(
        functools.partial(_mlp_kernel, nf=nf),
        grid=(t // bm, nf + 1),
        in_specs=[
            pl.BlockSpec((bm, d), lambda i, s: (i, 0)),
            pl.BlockSpec((1, d), lambda i, s: (0, 0)),
            pl.BlockSpec((None, d, bf), lambda i, s: (layer, 0, jnp.minimum(s, nf - 1))),
            pl.BlockSpec((None, bf, d), lambda i, s: (layer, jnp.maximum(s - 1, 0), 0)),
        ],
        out_specs=pl.BlockSpec((bm, d), lambda i, s: (i, 0)),
        out_shape=jax.ShapeDtypeStruct((t, d), F32),
        scratch_shapes=[pltpu.VMEM((bm, d), BF16), pltpu.VMEM((2, bm, bf), BF16)],
        compiler_params=_params(("parallel", "arbitrary")),
        name="relu2_mlp",
    )(x, g.reshape(1, d), w_in, w_out)


def _trunk(x, w):
    b, s, d = x.shape
    t = b * s
    n_heads = d // HEAD_DIM
    depth = w["norm_mix"].shape[0]
    cs, seq = _dft_tables(s)
    for layer in range(depth):
        j = layer // N_MIXERS
        if layer % N_MIXERS == 0:
            z = _chan_dft(x, w["norm_mix"][layer], cs)
            mix_in, w_o = _seq_dft(seq, z.reshape(b, DFT_RADIX, s // 2, d)), w["fourier_w_o"]
        else:
            h = _rmsnorm(x.reshape(t, d), w["norm_mix"][layer])
            bn_qkv = min(512, d)
            qkv = _matmul(h[None], w["na_w_qkv"], bm=1024, bn=bn_qkv, bk=d, out_dtype=BF16,
                          epilogue="headnorm", extra=w["na_qk_gain"], n_norm_tiles=2 * d // bn_qkv,
                          w_layer=j, extra_layer=j, name="qkv_proj")
            mix_in, w_o = _attention(qkv.reshape(b, s, 3 * d), w["na_bias"], j), w["na_w_o"]
        x = _matmul(mix_in.reshape(1, t, d), w_o, bm=1024, bn=512, bk=d, out_dtype=F32,
                    epilogue="residual", extra=x.reshape(1, t, d), w_layer=j, name="out_proj")
        x = _mlp(x.reshape(t, d), w["norm_ffn"][layer], w["ffn_w_in"], w["ffn_w_out"], layer).reshape(b, s, d)
    return x


def kernel(x_prompt, x_sample, norm_mix, norm_ffn, fourier_w_o, na_w_qkv, na_g_q, na_g_k, na_rpb, na_w_o,
           ffn_w_in, ffn_w_out):
    d = x_prompt.shape[-1]
    n_heads = d // HEAD_DIM
    n_na = na_w_qkv.shape[0]
    qk_gain = jnp.concatenate(
        [jnp.tile(na_g_q * (HEAD_DIM ** -0.5 * LOG2_E), (1, n_heads)), jnp.tile(na_g_k, (1, n_heads)),
         jnp.ones((n_na, d), F32)], axis=1).reshape(n_na, 1, 3 * d)
    w = {
        "norm_mix": norm_mix,
        "norm_ffn": norm_ffn,
        "fourier_w_o": fourier_w_o,
        "na_w_qkv": na_w_qkv,
        "na_qk_gain": qk_gain,
        "na_bias": _bias_table(na_rpb * LOG2_E),
        "na_w_o": na_w_o,
        "ffn_w_in": ffn_w_in.astype(BF16),
        "ffn_w_out": ffn_w_out.astype(BF16),
    }
    return _trunk(x_prompt, w), _trunk(x_sample, w)
```

```python
import functools
import math

import numpy as np
import jax
import jax.numpy as jnp
from jax import lax
from jax.experimental import pallas as pl
from jax.experimental.pallas import tpu as pltpu

GRID_W = 64
HEAD_DIM = 128
FOURIER_GROUP = 128
WIN_ROWS = 8
WIN_COLS = 16
N_MIXERS = 2
DFT_RADIX = 4
EPS = 1e-6
NEG_INF = -1e30
LOG2_E = math.log2(math.e)

V7X_VMEM_BYTES = 64 * 1024 * 1024
VMEM_LIMIT_BYTES = V7X_VMEM_BYTES - 2 * 1024 * 1024
LANES = 128
ATTN_Q_ROWS = 4
ATTN_K_TILES = 3
ATTN_GROUPS_PER_STEP = 4
ATTN_HEADS_PER_STEP = 2

F32 = jnp.float32
BF16 = jnp.bfloat16


def _params(semantics):
    return pltpu.CompilerParams(dimension_semantics=semantics, vmem_limit_bytes=VMEM_LIMIT_BYTES)


def _rms_scale(x):
    return lax.rsqrt(jnp.mean(x * x, axis=-1, keepdims=True) + EPS)


def _rmsnorm_kernel(x_ref, g_ref, o_ref):
    x = x_ref[...]
    o_ref[...] = (x * _rms_scale(x) * g_ref[...]).astype(o_ref.dtype)


def _rmsnorm(x, g, *, bm=256):
    t, d = x.shape
    bm = min(bm, t)
    return pl.pallas_call(
        _rmsnorm_kernel,
        grid=(t // bm,),
        in_specs=[pl.BlockSpec((bm, d), lambda i: (i, 0)), pl.BlockSpec((1, d), lambda i: (0, 0))],
        out_specs=pl.BlockSpec((bm, d), lambda i: (i, 0)),
        out_shape=jax.ShapeDtypeStruct((t, d), BF16),
        compiler_params=_params(("parallel",)),
        name="rmsnorm",
    )(x, g.reshape(1, d))


def _mm_kernel(*refs, nk, epilogue, n_norm_tiles):
    a_ref, w_ref = refs[0], refs[1]
    extra_ref = refs[2] if epilogue in ("residual", "headnorm") else None
    o_ref = refs[3] if extra_ref is not None else refs[2]
    acc_ref = refs[-1] if nk > 1 else None
    j = pl.program_id(2)
    k = pl.program_id(3)

    def finish(acc):
        if epilogue == "cast":
            o_ref[...] = acc.astype(o_ref.dtype)
        elif epilogue == "residual":
            o_ref[...] = (extra_ref[...] + acc).astype(o_ref.dtype)
        else:
            @pl.when(j < n_norm_tiles)
            def _():
                for h in range(acc.shape[1] // HEAD_DIM):
                    sl = slice(h * HEAD_DIM, (h + 1) * HEAD_DIM)
                    a = acc[:, sl]
                    o_ref[:, sl] = (a * _rms_scale(a) * extra_ref[:, sl]).astype(o_ref.dtype)

            @pl.when(j >= n_norm_tiles)
            def _():
                o_ref[...] = acc.astype(o_ref.dtype)

    part = jnp.dot(a_ref[...], w_ref[...], preferred_element_type=F32)
    if nk == 1:
        finish(part)
    else:
        @pl.when(k == 0)
        def _():
            acc_ref[...] = part

        @pl.when(jnp.logical_and(k > 0, k < nk - 1))
        def _():
            acc_ref[...] += part

        @pl.when(k == nk - 1)
        def _():
            finish(acc_ref[...] + part)


def _matmul(a, w, *, bm, bn, bk, out_dtype, epilogue="cast", extra=None, n_norm_tiles=0, w_layer=None,
            extra_layer=None, name):
    ba, m, kdim = a.shape
    bw, _, n = w.shape
    if w_layer is not None:
        bw = 1
    nb = max(ba, bw)
    bm, bn, bk = min(bm, m), min(bn, n), min(bk, kdim)
    nk = kdim // bk
    a_map = (lambda b, i, j, k: (b, i, k)) if ba > 1 else (lambda b, i, j, k: (0, i, k))
    w_first = 0 if w_layer is None else w_layer
    w_map = (lambda b, i, j, k: (b, k, j)) if bw > 1 else (lambda b, i, j, k: (w_first, k, j))
    in_specs = [pl.BlockSpec((None, bm, bk), a_map), pl.BlockSpec((None, bk, bn), w_map)]
    args = [a, w]
    if epilogue == "residual":
        in_specs.append(pl.BlockSpec((None, bm, bn), lambda b, i, j, k: (b, i, j)))
        args.append(extra)
    elif epilogue == "headnorm":
        in_specs.append(pl.BlockSpec((None, 1, bn), lambda b, i, j, k: (extra_layer, 0, j)))
        args.append(extra)
    return pl.pallas_call(
        functools.partial(_mm_kernel, nk=nk, epilogue=epilogue, n_norm_tiles=n_norm_tiles),
        grid=(nb, m // bm, n // bn, nk),
        in_specs=in_specs,
        out_specs=pl.BlockSpec((None, bm, bn), lambda b, i, j, k: (b, i, j)),
        out_shape=jax.ShapeDtypeStruct((nb, m, n), out_dtype),
        scratch_shapes=[pltpu.VMEM((bm, bn), F32)] if nk > 1 else [],
        compiler_params=_params(("parallel", "parallel", "parallel", "arbitrary")),
        name=name,
    )(*args)


def _chan_dft_kernel(x0_ref, x1_ref, x2_ref, x3_ref, g_ref, cs_ref, z_ref):
    h0, h1, h2, h3 = ((x_ref[...] * _rms_scale(x_ref[...]) * g_ref[...]) for x_ref in (x0_ref, x1_ref, x2_ref, x3_ref))
    even, odd = h0 + h2, h1 + h3
    lhs = jnp.concatenate([even + odd, even - odd, h0 - h2, h1 - h3], axis=0).astype(BF16)
    bm = h0.shape[0]
    for grp in range(h0.shape[1] // FOURIER_GROUP):
        sl = slice(grp * FOURIER_GROUP, (grp + 1) * FOURIER_GROUP)
        y = jnp.dot(lhs[:, sl], cs_ref[...], preferred_element_type=F32)
        yc, ys = y[:, :FOURIER_GROUP], y[:, FOURIER_GROUP:]
        ac, a_s, bc, b_s = yc[2 * bm:3 * bm], ys[2 * bm:3 * bm], yc[3 * bm:], ys[3 * bm:]
        z_ref[0, 0, :, sl] = yc[:bm].astype(z_ref.dtype)
        z_ref[0, 1, :, sl] = ys[:bm].astype(z_ref.dtype)
        z_ref[2, 0, :, sl] = yc[bm:2 * bm].astype(z_ref.dtype)
        z_ref[2, 1, :, sl] = ys[bm:2 * bm].astype(z_ref.dtype)
        z_ref[1, 0, :, sl] = (ac - b_s).astype(z_ref.dtype)
        z_ref[1, 1, :, sl] = (a_s + bc).astype(z_ref.dtype)
        z_ref[3, 0, :, sl] = (ac + b_s).astype(z_ref.dtype)
        z_ref[3, 1, :, sl] = (a_s - bc).astype(z_ref.dtype)


def _largest_tile(n, cap):
    tile = 16
    while tile * 2 <= cap and n % (tile * 2) == 0:
        tile *= 2
    assert n % tile == 0
    return tile


def _chan_dft(x, g, cs):
    b, s, d = x.shape
    quarter = s // DFT_RADIX
    bm = _largest_tile(quarter, 128)
    tiles = quarter // bm
    x_specs = [pl.BlockSpec((None, bm, d), functools.partial(lambda bi, i, q: (bi, q * tiles + i, 0), q=q))
               for q in range(DFT_RADIX)]
    return pl.pallas_call(
        _chan_dft_kernel,
        grid=(b, tiles),
        in_specs=x_specs + [
            pl.BlockSpec((1, d), lambda bi, i: (0, 0)),
            pl.BlockSpec((FOURIER_GROUP, 2 * FOURIER_GROUP), lambda bi, i: (0, 0)),
        ],
        out_specs=pl.BlockSpec((None, DFT_RADIX, 2, bm, d), lambda bi, i: (bi, 0, 0, i, 0)),
        out_shape=jax.ShapeDtypeStruct((b, DFT_RADIX, 2, quarter, d), BF16),
        compiler_params=_params(("parallel", "parallel")),
        name="chan_dft",
    )(x, x, x, x, g.reshape(1, d), cs)


def _seq_dft_kernel(f_ref, z_ref, o_ref):
    o_ref[...] = jnp.dot(f_ref[...], z_ref[...], preferred_element_type=F32).astype(o_ref.dtype)


def _seq_dft(f, z):
    b, _, half, d = z.shape
    quarter = half // 2
    bm, bn = _largest_tile(quarter, 1024), _largest_tile(d, 2048)
    if quarter <= 1024 and quarter % 16 == 0:
        bm = quarter
    nj = d // bn
    return pl.pallas_call(
        _seq_dft_kernel,
        grid=(b, DFT_RADIX, quarter // bm, nj),
        in_specs=[
            pl.BlockSpec((None, bm, half), lambda bi, r, i, j: (r, i, 0)),
            pl.BlockSpec((None, None, half, bn), lambda bi, r, i, j: (bi, r, 0, j)),
        ],
        out_specs=pl.BlockSpec((None, bm, bn), lambda bi, r, i, j: (bi, i, r * nj + j)),
        out_shape=jax.ShapeDtypeStruct((b, quarter, DFT_RADIX * d), BF16),
        compiler_params=_params(("parallel", "parallel", "parallel", "parallel")),
        name="seq_dft",
    )(f, z)


def _dft_tables(s):
    def angles(freq, n, period):
        return ((freq * n) % period).astype(F32) * (2.0 * jnp.pi / period)

    iota = lambda shape, axis: lax.broadcasted_iota(jnp.int32, shape, axis)
    grp = (FOURIER_GROUP, FOURIER_GROUP)
    ang = angles(iota(grp, 0), iota(grp, 1), FOURIER_GROUP)
    cs = jnp.concatenate([jnp.cos(ang), jnp.sin(ang)], axis=1) * FOURIER_GROUP ** -0.5
    quarter = s // DFT_RADIX
    shape = (DFT_RADIX, quarter, quarter)
    ang = angles(DFT_RADIX * iota(shape, 1) + iota(shape, 0), iota(shape, 2), s)
    seq = jnp.concatenate([jnp.cos(ang), -jnp.sin(ang)], axis=2) * s ** -0.5
    return cs.astype(BF16), seq.astype(BF16)


def _attn_kernel(q_ref, k_ref, v_ref, bias_ref, o_ref, *, n_groups, heads, groups_per_step):
    gq = ATTN_Q_ROWS * GRID_W
    gk = ATTN_K_TILES * gq

    def body(it, carry):
        groups = [it * groups_per_step + u for u in range(groups_per_step)]
        work = [(g, h) for g in groups for h in range(heads)]
        scores = []
        for g, h in work:
            lanes = slice(h * HEAD_DIM, (h + 1) * HEAD_DIM)
            k_start = pl.multiple_of(jnp.clip(g - 1, 0, n_groups - ATTN_K_TILES) * gq, gq)
            q = q_ref[pl.ds(pl.multiple_of(g * gq, gq), gq), lanes]
            k = k_ref[pl.ds(k_start, gk), lanes]
            kind = jnp.where(g == 0, 0, jnp.where(g == n_groups - 1, 2, 1))
            s = lax.dot_general(q, k, (((1,), (1,)), ((), ())), preferred_element_type=F32)
            scores.append(s + bias_ref[h, kind])
        probs, denoms = [], []
        for s in scores:
            p = jnp.exp2(s - jnp.max(s, axis=-1, keepdims=True))
            denoms.append(jnp.sum(p, axis=-1, keepdims=True))
            probs.append(p.astype(BF16))
        for (g, h), p, denom in zip(work, probs, denoms):
            lanes = slice(h * HEAD_DIM, (h + 1) * HEAD_DIM)
            k_start = pl.multiple_of(jnp.clip(g - 1, 0, n_groups - ATTN_K_TILES) * gq, gq)
            o = jnp.dot(p, v_ref[pl.ds(k_start, gk), lanes], preferred_element_type=F32)
            o_ref[pl.ds(pl.multiple_of(g * gq, gq), gq), lanes] = (o / denom).astype(o_ref.dtype)
        return carry

    lax.fori_loop(0, n_groups // groups_per_step, body, 0)


def _attention(qkv, bias, layer):
    b, s, d3 = qkv.shape
    d = d3 // 3
    heads = ATTN_HEADS_PER_STEP
    n_head_blocks = d // (heads * HEAD_DIM)
    n_groups = s // (ATTN_Q_ROWS * GRID_W)
    assert s % (ATTN_Q_ROWS * GRID_W) == 0 and n_groups >= ATTN_K_TILES
    groups_per_step = math.gcd(n_groups, ATTN_GROUPS_PER_STEP)
    width = heads * HEAD_DIM
    return pl.pallas_call(
        functools.partial(_attn_kernel, n_groups=n_groups, heads=heads, groups_per_step=groups_per_step),
        grid=(n_head_blocks, b),
        in_specs=[
            pl.BlockSpec((None, s, width), lambda h, bi: (bi, 0, h)),
            pl.BlockSpec((None, s, width), lambda h, bi: (bi, 0, n_head_blocks + h)),
            pl.BlockSpec((None, s, width), lambda h, bi: (bi, 0, 2 * n_head_blocks + h)),
            pl.BlockSpec((None, heads) + bias.shape[2:], lambda h, bi: (layer, h, 0, 0, 0)),
        ],
        out_specs=pl.BlockSpec((None, s, width), lambda h, bi: (bi, 0, h)),
        out_shape=jax.ShapeDtypeStruct((b, s, d), BF16),
        compiler_params=_params(("parallel", "parallel")),
        name="nbr_attention",
    )(qkv, qkv, qkv, bias)


def _bias_table_kernel(rpb_ref, o_ref):
    n_key_rows = ATTN_K_TILES * ATTN_Q_ROWS
    n_rel_rows = 2 * WIN_ROWS - 1
    shape = (GRID_W, LANES)
    c = lax.broadcasted_iota(jnp.int32, shape, 0)
    lane = lax.broadcasted_iota(jnp.int32, shape, 1)
    low = lane < GRID_W
    kc = lane & (GRID_W - 1)
    col_start = jnp.clip(c - WIN_COLS // 2, 0, GRID_W - WIN_COLS)
    col_ok = jnp.logical_and(kc >= col_start, kc < col_start + WIN_COLS)
    neg = jnp.full(shape, NEG_INF, F32)

    def toep(rho, extra_shift):
        row = jnp.broadcast_to(rpb_ref[rho:rho + 1, :], shape)
        return pltpu.roll(row, extra_shift, 1, stride=1, stride_axis=0)

    def pair(rho):
        lo = toep(rho, 0) if 0 <= rho < n_rel_rows else neg
        hi = toep(rho + 1, GRID_W) if 0 <= rho + 1 < n_rel_rows else neg
        return jnp.where(low, lo, hi)

    kinds = ((0, lambda t: 0), (ATTN_Q_ROWS, lambda t: t), (2 * ATTN_Q_ROWS, lambda t: ATTN_Q_ROWS))
    for kind, (first_q_row, win_start) in enumerate(kinds):
        for t in range(ATTN_Q_ROWS):
            w0 = win_start(t)
            for jp in range(n_key_rows // 2):
                ok_lo = w0 <= 2 * jp < w0 + WIN_ROWS
                ok_hi = w0 <= 2 * jp + 1 < w0 + WIN_ROWS
                if ok_lo or ok_hi:
                    rows_ok = col_ok if (ok_lo and ok_hi) else jnp.logical_and(col_ok, low if ok_lo else ~low)
                    blk = jnp.where(rows_ok, pair(2 * jp - (first_q_row + t) + WIN_ROWS - 1), neg)
                else:
                    blk = neg
                o_ref[kind, t * GRID_W:(t + 1) * GRID_W, jp * LANES:(jp + 1) * LANES] = blk


def _bias_table(rpb):
    n_layers, n_heads, n_rel_rows, n_rel_cols = rpb.shape
    rows = jnp.pad(rpb, ((0, 0), (0, 0), (0, 16 - n_rel_rows), (0, LANES - n_rel_cols)))
    rows = jnp.roll(rows, -(WIN_COLS - 1), axis=-1)
    table_shape = (3, ATTN_Q_ROWS * GRID_W, ATTN_K_TILES * ATTN_Q_ROWS * GRID_W)
    return pl.pallas_call(
        _bias_table_kernel,
        grid=(n_layers, n_heads),
        in_specs=[pl.BlockSpec((None, None, 16, LANES), lambda l, h: (l, h, 0, 0))],
        out_specs=pl.BlockSpec((None, None) + table_shape, lambda l, h: (l, h, 0, 0, 0)),
        out_shape=jax.ShapeDtypeStruct((n_layers, n_heads) + table_shape, F32),
        compiler_params=_params(("parallel", "parallel")),
        name="bias_table",
    )(rows)


def _mlp_kernel(x_ref, g_ref, win_ref, wout_ref, o_ref, h_ref, u_ref, *, nf):
    s = pl.program_id(1)

    def up(slot):
        u = jnp.maximum(jnp.dot(h_ref[...], win_ref[...], preferred_element_type=F32), 0.0)
        u_ref[slot] = (u * u).astype(BF16)

    def down(slot):
        o_ref[...] += jnp.dot(u_ref[slot], wout_ref[...], preferred_element_type=F32)

    @pl.when(s == 0)
    def _():
        x = x_ref[...]
        h_ref[...] = (x * _rms_scale(x) * g_ref[...]).astype(BF16)
        o_ref[...] = x
        up(0)

    for parity in (0, 1):
        @pl.when(jnp.logical_and(jnp.logical_and(s > 0, s < nf), s % 2 == parity))
        def _():
            down(1 - parity)
            up(parity)

    @pl.when(s == nf)
    def _():
        down((nf - 1) % 2)


def _mlp(x, g, w_in, w_out, layer, *, bm=1024, bf=512):
    t, d = x.shape
    f = w_in.shape[2]
    bm, bf = min(bm, t), min(bf, f)
    nf = f // bf
    once_per_row_tile = dict(pipeline_mode=pl.Buffered(1))
    return pl.pallas_call(
        functools.partial(_mlp_kernel, nf=nf),
        grid=(t // bm, nf + 1),
        in_specs=[
            pl.BlockSpec((bm, d), lambda i, s: (i, 0), **once_per_row_tile),
            pl.BlockSpec((1, d), lambda i, s: (0, 0)),
            pl.BlockSpec((None, d, bf), lambda i, s: (layer, 0, jnp.minimum(s, nf - 1))),
            pl.BlockSpec((None, bf, d), lambda i, s: (layer, jnp.maximum(s - 1, 0), 0)),
        ],
        out_specs=pl.BlockSpec((bm, d), lambda i, s: (i, 0), **once_per_row_tile),
        out_shape=jax.ShapeDtypeStruct((t, d), F32),
        scratch_shapes=[pltpu.VMEM((bm, d), BF16), pltpu.VMEM((2, bm, bf), BF16)],
        compiler_params=_params(("parallel", "arbitrary")),
        name="relu2_mlp",
    )(x, g.reshape(1, d), w_in, w_out)


def _trunk(x, w):
    b, s, d = x.shape
    t = b * s
    n_heads = d // HEAD_DIM
    depth = w["norm_mix"].shape[0]
    cs, seq = _dft_tables(s)
    for layer in range(depth):
        j = layer // N_MIXERS
        if layer % N_MIXERS == 0:
            z = _chan_dft(x, w["norm_mix"][layer], cs)
            mix_in, w_o = _seq_dft(seq, z.reshape(b, DFT_RADIX, s // 2, d)), w["fourier_w_o"]
        else:
            h = _rmsnorm(x.reshape(t, d), w["norm_mix"][layer])
            bn_qkv = min(1024, d)
            qkv = _matmul(h[None], w["na_w_qkv"], bm=1024, bn=bn_qkv, bk=d, out_dtype=BF16,
                          epilogue="headnorm", extra=w["na_qk_gain"], n_norm_tiles=2 * d // bn_qkv,
                          w_layer=j, extra_layer=j, name="qkv_proj")
            mix_in, w_o = _attention(qkv.reshape(b, s, 3 * d), w["na_bias"], j), w["na_w_o"]
        x = _matmul(mix_in.reshape(1, t, d), w_o, bm=1024, bn=1024, bk=d, out_dtype=F32,
                    epilogue="residual", extra=x.reshape(1, t, d), w_layer=j, name="out_proj")
        x = _mlp(x.reshape(t, d), w["norm_ffn"][layer], w["ffn_w_in"], w["ffn_w_out"], layer).reshape(b, s, d)
    return x


def kernel(x_prompt, x_sample, norm_mix, norm_ffn, fourier_w_o, na_w_qkv, na_g_q, na_g_k, na_rpb, na_w_o,
           ffn_w_in, ffn_w_out):
    d = x_prompt.shape[-1]
    n_heads = d // HEAD_DIM
    n_na = na_w_qkv.shape[0]
    qk_gain = jnp.concatenate(
        [jnp.tile(na_g_q * (HEAD_DIM ** -0.5 * LOG2_E), (1, n_heads)), jnp.tile(na_g_k, (1, n_heads)),
         jnp.ones((n_na, d), F32)], axis=1).reshape(n_na, 1, 3 * d)
    w = {
        "norm_mix": norm_mix,
        "norm_ffn": norm_ffn,
        "fourier_w_o": fourier_w_o.astype(BF16),
        "na_w_qkv": na_w_qkv.astype(BF16),
        "na_qk_gain": qk_gain,
        "na_bias": _bias_table(na_rpb * LOG2_E),
        "na_w_o": na_w_o.astype(BF16),
        "ffn_w_in": ffn_w_in.astype(BF16),
        "ffn_w_out": ffn_w_out.astype(BF16),
    }
    return _trunk(x_prompt, w), _trunk(x_sample, w)
```

```python
import functools
import math

import numpy as np
import jax
import jax.numpy as jnp
from jax import lax
from jax.experimental import pallas as pl
from jax.experimental.pallas import tpu as pltpu

GRID_W = 64
HEAD_DIM = 128
FOURIER_GROUP = 128
WIN_ROWS = 8
WIN_COLS = 16
N_MIXERS = 2
DFT_RADIX = 4
EPS = 1e-6
NEG_INF = -1e30
LOG2_E = math.log2(math.e)

V7X_VMEM_BYTES = 64 * 1024 * 1024
VMEM_LIMIT_BYTES = V7X_VMEM_BYTES - 2 * 1024 * 1024
LANES = 128
ATTN_Q_ROWS = 4
ATTN_K_TILES = 3
ATTN_GROUPS_PER_STEP = 4
ATTN_HEADS_PER_STEP = 2

F32 = jnp.float32
BF16 = jnp.bfloat16


def _params(semantics):
    return pltpu.CompilerParams(dimension_semantics=semantics, vmem_limit_bytes=VMEM_LIMIT_BYTES)


def _rms_scale(x):
    return lax.rsqrt(jnp.mean(x * x, axis=-1, keepdims=True) + EPS)


def _rmsnorm_kernel(x_ref, g_ref, o_ref):
    x = x_ref[...]
    o_ref[...] = (x * _rms_scale(x) * g_ref[...]).astype(o_ref.dtype)


def _rmsnorm(x, g, *, bm=256):
    t, d = x.shape
    bm = min(bm, t)
    return pl.pallas_call(
        _rmsnorm_kernel,
        grid=(t // bm,),
        in_specs=[pl.BlockSpec((bm, d), lambda i: (i, 0)), pl.BlockSpec((1, d), lambda i: (0, 0))],
        out_specs=pl.BlockSpec((bm, d), lambda i: (i, 0)),
        out_shape=jax.ShapeDtypeStruct((t, d), BF16),
        compiler_params=_params(("parallel",)),
        name="rmsnorm",
    )(x, g.reshape(1, d))


def _mm_kernel(*refs, nk, epilogue, n_norm_tiles):
    a_ref, w_ref = refs[0], refs[1]
    extra_ref = refs[2] if epilogue in ("residual", "headnorm") else None
    o_ref = refs[3] if extra_ref is not None else refs[2]
    acc_ref = refs[-1] if nk > 1 else None
    j = pl.program_id(2)
    k = pl.program_id(3)

    def finish(acc):
        if epilogue == "cast":
            o_ref[...] = acc.astype(o_ref.dtype)
        elif epilogue == "residual":
            o_ref[...] = (extra_ref[...] + acc).astype(o_ref.dtype)
        else:
            @pl.when(j < n_norm_tiles)
            def _():
                for h in range(acc.shape[1] // HEAD_DIM):
                    sl = slice(h * HEAD_DIM, (h + 1) * HEAD_DIM)
                    a = acc[:, sl]
                    o_ref[:, sl] = (a * _rms_scale(a) * extra_ref[:, sl]).astype(o_ref.dtype)

            @pl.when(j >= n_norm_tiles)
            def _():
                o_ref[...] = acc.astype(o_ref.dtype)

    part = jnp.dot(a_ref[...], w_ref[...], preferred_element_type=F32)
    if nk == 1:
        finish(part)
    else:
        @pl.when(k == 0)
        def _():
            acc_ref[...] = part

        @pl.when(jnp.logical_and(k > 0, k < nk - 1))
        def _():
            acc_ref[...] += part

        @pl.when(k == nk - 1)
        def _():
            finish(acc_ref[...] + part)


def _matmul(a, w, *, bm, bn, bk, out_dtype, epilogue="cast", extra=None, n_norm_tiles=0, w_layer=None,
            extra_layer=None, name):
    ba, m, kdim = a.shape
    bw, _, n = w.shape
    if w_layer is not None:
        bw = 1
    nb = max(ba, bw)
    bm, bn, bk = min(bm, m), min(bn, n), min(bk, kdim)
    nk = kdim // bk
    a_map = (lambda b, i, j, k: (b, i, k)) if ba > 1 else (lambda b, i, j, k: (0, i, k))
    w_first = 0 if w_layer is None else w_layer
    w_map = (lambda b, i, j, k: (b, k, j)) if bw > 1 else (lambda b, i, j, k: (w_first, k, j))
    in_specs = [pl.BlockSpec((None, bm, bk), a_map), pl.BlockSpec((None, bk, bn), w_map)]
    args = [a, w]
    if epilogue == "residual":
        in_specs.append(pl.BlockSpec((None, bm, bn), lambda b, i, j, k: (b, i, j)))
        args.append(extra)
    elif epilogue == "headnorm":
        in_specs.append(pl.BlockSpec((None, 1, bn), lambda b, i, j, k: (extra_layer, 0, j)))
        args.append(extra)
    return pl.pallas_call(
        functools.partial(_mm_kernel, nk=nk, epilogue=epilogue, n_norm_tiles=n_norm_tiles),
        grid=(nb, m // bm, n // bn, nk),
        in_specs=in_specs,
        out_specs=pl.BlockSpec((None, bm, bn), lambda b, i, j, k: (b, i, j)),
        out_shape=jax.ShapeDtypeStruct((nb, m, n), out_dtype),
        scratch_shapes=[pltpu.VMEM((bm, bn), F32)] if nk > 1 else [],
        compiler_params=_params(("parallel", "parallel", "parallel", "arbitrary")),
        name=name,
    )(*args)


def _chan_dft_kernel(x0_ref, x1_ref, x2_ref, x3_ref, g_ref, cs_ref, z_ref):
    h0, h1, h2, h3 = ((x_ref[...] * _rms_scale(x_ref[...]) * g_ref[...]) for x_ref in (x0_ref, x1_ref, x2_ref, x3_ref))
    even, odd = h0 + h2, h1 + h3
    lhs = jnp.concatenate([even + odd, even - odd, h0 - h2, h1 - h3], axis=0).astype(BF16)
    bm = h0.shape[0]
    for grp in range(h0.shape[1] // FOURIER_GROUP):
        sl = slice(grp * FOURIER_GROUP, (grp + 1) * FOURIER_GROUP)
        y = jnp.dot(lhs[:, sl], cs_ref[...], preferred_element_type=F32)
        yc, ys = y[:, :FOURIER_GROUP], y[:, FOURIER_GROUP:]
        ac, a_s, bc, b_s = yc[2 * bm:3 * bm], ys[2 * bm:3 * bm], yc[3 * bm:], ys[3 * bm:]
        z_ref[0, 0, :, sl] = yc[:bm].astype(z_ref.dtype)
        z_ref[0, 1, :, sl] = ys[:bm].astype(z_ref.dtype)
        z_ref[2, 0, :, sl] = yc[bm:2 * bm].astype(z_ref.dtype)
        z_ref[2, 1, :, sl] = ys[bm:2 * bm].astype(z_ref.dtype)
        z_ref[1, 0, :, sl] = (ac - b_s).astype(z_ref.dtype)
        z_ref[1, 1, :, sl] = (a_s + bc).astype(z_ref.dtype)
        z_ref[3, 0, :, sl] = (ac + b_s).astype(z_ref.dtype)
        z_ref[3, 1, :, sl] = (a_s - bc).astype(z_ref.dtype)


def _largest_tile(n, cap):
    tile = 16
    while tile * 2 <= cap and n % (tile * 2) == 0:
        tile *= 2
    assert n % tile == 0
    return tile


def _chan_dft(x, g, cs):
    b, s, d = x.shape
    quarter = s // DFT_RADIX
    bm = _largest_tile(quarter, 128)
    tiles = quarter // bm
    x_specs = [pl.BlockSpec((None, bm, d), functools.partial(lambda bi, i, q: (bi, q * tiles + i, 0), q=q))
               for q in range(DFT_RADIX)]
    return pl.pallas_call(
        _chan_dft_kernel,
        grid=(b, tiles),
        in_specs=x_specs + [
            pl.BlockSpec((1, d), lambda bi, i: (0, 0)),
            pl.BlockSpec((FOURIER_GROUP, 2 * FOURIER_GROUP), lambda bi, i: (0, 0)),
        ],
        out_specs=pl.BlockSpec((None, DFT_RADIX, 2, bm, d), lambda bi, i: (bi, 0, 0, i, 0)),
        out_shape=jax.ShapeDtypeStruct((b, DFT_RADIX, 2, quarter, d), BF16),
        compiler_params=_params(("parallel", "parallel")),
        name="chan_dft",
    )(x, x, x, x, g.reshape(1, d), cs)


def _seq_dft_kernel(f_ref, z_ref, o_ref):
    o_ref[...] = jnp.dot(f_ref[...], z_ref[...], preferred_element_type=F32).astype(o_ref.dtype)


def _seq_dft(f, z):
    b, _, half, d = z.shape
    quarter = half // 2
    bm, bn = _largest_tile(quarter, 1024), _largest_tile(d, 2048)
    if quarter <= 1024 and quarter % 16 == 0:
        bm = quarter
    nj = d // bn
    return pl.pallas_call(
        _seq_dft_kernel,
        grid=(b, DFT_RADIX, quarter // bm, nj),
        in_specs=[
            pl.BlockSpec((None, bm, half), lambda bi, r, i, j: (r, i, 0)),
            pl.BlockSpec((None, None, half, bn), lambda bi, r, i, j: (bi, r, 0, j)),
        ],
        out_specs=pl.BlockSpec((None, bm, bn), lambda bi, r, i, j: (bi, i, r * nj + j)),
        out_shape=jax.ShapeDtypeStruct((b, quarter, DFT_RADIX * d), BF16),
        compiler_params=_params(("parallel", "parallel", "parallel", "parallel")),
        name="seq_dft",
    )(f, z)


def _dft_tables(s):
    def angles(freq, n, period):
        return ((freq * n) % period).astype(F32) * (2.0 * jnp.pi / period)

    iota = lambda shape, axis: lax.broadcasted_iota(jnp.int32, shape, axis)
    grp = (FOURIER_GROUP, FOURIER_GROUP)
    ang = angles(iota(grp, 0), iota(grp, 1), FOURIER_GROUP)
    cs = jnp.concatenate([jnp.cos(ang), jnp.sin(ang)], axis=1) * FOURIER_GROUP ** -0.5
    quarter = s // DFT_RADIX
    shape = (DFT_RADIX, quarter, quarter)
    ang = angles(DFT_RADIX * iota(shape, 1) + iota(shape, 0), iota(shape, 2), s)
    seq = jnp.concatenate([jnp.cos(ang), -jnp.sin(ang)], axis=2) * s ** -0.5
    return cs.astype(BF16), seq.astype(BF16)


def _attn_kernel(q_ref, k_ref, v_ref, bias_ref, o_ref, *, n_groups, heads, groups_per_step):
    gq = ATTN_Q_ROWS * GRID_W
    gk = ATTN_K_TILES * gq

    def body(it, carry):
        groups = [it * groups_per_step + u for u in range(groups_per_step)]
        work = [(g, h) for g in groups for h in range(heads)]
        scores = []
        for g, h in work:
            lanes = slice(h * HEAD_DIM, (h + 1) * HEAD_DIM)
            k_start = pl.multiple_of(jnp.clip(g - 1, 0, n_groups - ATTN_K_TILES) * gq, gq)
            q = q_ref[pl.ds(pl.multiple_of(g * gq, gq), gq), lanes]
            k = k_ref[pl.ds(k_start, gk), lanes]
            kind = jnp.where(g == 0, 0, jnp.where(g == n_groups - 1, 2, 1))
            s = lax.dot_general(q, k, (((1,), (1,)), ((), ())), preferred_element_type=F32)
            scores.append(s + bias_ref[h, kind])
        probs, denoms = [], []
        for s in scores:
            p = jnp.exp2(s - jnp.max(s, axis=-1, keepdims=True))
            denoms.append(jnp.sum(p, axis=-1, keepdims=True))
            probs.append(p.astype(BF16))
        for (g, h), p, denom in zip(work, probs, denoms):
            lanes = slice(h * HEAD_DIM, (h + 1) * HEAD_DIM)
            k_start = pl.multiple_of(jnp.clip(g - 1, 0, n_groups - ATTN_K_TILES) * gq, gq)
            o = jnp.dot(p, v_ref[pl.ds(k_start, gk), lanes], preferred_element_type=F32)
            o_ref[pl.ds(pl.multiple_of(g * gq, gq), gq), lanes] = (o / denom).astype(o_ref.dtype)
        return carry

    lax.fori_loop(0, n_groups // groups_per_step, body, 0)


def _attention(qkv, bias, layer):
    b, s, d3 = qkv.shape
    d = d3 // 3
    heads = ATTN_HEADS_PER_STEP
    n_head_blocks = d // (heads * HEAD_DIM)
    n_groups = s // (ATTN_Q_ROWS * GRID_W)
    assert s % (ATTN_Q_ROWS * GRID_W) == 0 and n_groups >= ATTN_K_TILES
    groups_per_step = math.gcd(n_groups, ATTN_GROUPS_PER_STEP)
    width = heads * HEAD_DIM
    return pl.pallas_call(
        functools.partial(_attn_kernel, n_groups=n_groups, heads=heads, groups_per_step=groups_per_step),
        grid=(n_head_blocks, b),
        in_specs=[
            pl.BlockSpec((None, s, width), lambda h, bi: (bi, 0, h)),
            pl.BlockSpec((None, s, width), lambda h, bi: (bi, 0, n_head_blocks + h)),
            pl.BlockSpec((None, s, width), lambda h, bi: (bi, 0, 2 * n_head_blocks + h)),
            pl.BlockSpec((None, heads) + bias.shape[2:], lambda h, bi: (layer, h, 0, 0, 0)),
        ],
        out_specs=pl.BlockSpec((None, s, width), lambda h, bi: (bi, 0, h)),
        out_shape=jax.ShapeDtypeStruct((b, s, d), BF16),
        compiler_params=_params(("parallel", "parallel")),
        name="nbr_attention",
    )(qkv, qkv, qkv, bias)


def _bias_table_kernel(rpb_ref, o_ref):
    n_key_rows = ATTN_K_TILES * ATTN_Q_ROWS
    n_rel_rows = 2 * WIN_ROWS - 1
    shape = (GRID_W, LANES)
    c = lax.broadcasted_iota(jnp.int32, shape, 0)
    lane = lax.broadcasted_iota(jnp.int32, shape, 1)
    low = lane < GRID_W
    kc = lane & (GRID_W - 1)
    col_start = jnp.clip(c - WIN_COLS // 2, 0, GRID_W - WIN_COLS)
    col_ok = jnp.logical_and(kc >= col_start, kc < col_start + WIN_COLS)
    neg = jnp.full(shape, NEG_INF, F32)

    def toep(rho, extra_shift):
        row = jnp.broadcast_to(rpb_ref[rho:rho + 1, :], shape)
        return pltpu.roll(row, extra_shift, 1, stride=1, stride_axis=0)

    def pair(rho):
        lo = toep(rho, 0) if 0 <= rho < n_rel_rows else neg
        hi = toep(rho + 1, GRID_W) if 0 <= rho + 1 < n_rel_rows else neg
        return jnp.where(low, lo, hi)

    kinds = ((0, lambda t: 0), (ATTN_Q_ROWS, lambda t: t), (2 * ATTN_Q_ROWS, lambda t: ATTN_Q_ROWS))
    for kind, (first_q_row, win_start) in enumerate(kinds):
        for t in range(ATTN_Q_ROWS):
            w0 = win_start(t)
            for jp in range(n_key_rows // 2):
                ok_lo = w0 <= 2 * jp < w0 + WIN_ROWS
                ok_hi = w0 <= 2 * jp + 1 < w0 + WIN_ROWS
                if ok_lo or ok_hi:
                    rows_ok = col_ok if (ok_lo and ok_hi) else jnp.logical_and(col_ok, low if ok_lo else ~low)
                    blk = jnp.where(rows_ok, pair(2 * jp - (first_q_row + t) + WIN_ROWS - 1), neg)
                else:
                    blk = neg
                o_ref[kind, t * GRID_W:(t + 1) * GRID_W, jp * LANES:(jp + 1) * LANES] = blk


def _bias_table(rpb):
    n_layers, n_heads, n_rel_rows, n_rel_cols = rpb.shape
    rows = jnp.pad(rpb, ((0, 0), (0, 0), (0, 16 - n_rel_rows), (0, LANES - n_rel_cols)))
    rows = jnp.roll(rows, -(WIN_COLS - 1), axis=-1)
    table_shape = (3, ATTN_Q_ROWS * GRID_W, ATTN_K_TILES * ATTN_Q_ROWS * GRID_W)
    return pl.pallas_call(
        _bias_table_kernel,
        grid=(n_layers, n_heads),
        in_specs=[pl.BlockSpec((None, None, 16, LANES), lambda l, h: (l, h, 0, 0))],
        out_specs=pl.BlockSpec((None, None) + table_shape, lambda l, h: (l, h, 0, 0, 0)),
        out_shape=jax.ShapeDtypeStruct((n_layers, n_heads) + table_shape, F32),
        compiler_params=_params(("parallel", "parallel")),
        name="bias_table",
    )(rows)


def _mlp_kernel(x_hbm, g_ref, win_ref, wout_ref, o_hbm, x_buf, acc_ref, h_ref, u_ref, x_sem, o_sem, *, nf, bm):
    i, s = pl.program_id(0), pl.program_id(1)
    n_row_tiles = pl.num_programs(0)

    def x_copy(tile):
        return pltpu.make_async_copy(x_hbm.at[pl.ds(tile * bm, bm)], x_buf, x_sem)

    def out_copy(tile):
        return pltpu.make_async_copy(acc_ref, o_hbm.at[pl.ds(tile * bm, bm)], o_sem)

    def up(slot):
        u = jnp.maximum(jnp.dot(h_ref[...], win_ref[...], preferred_element_type=F32), 0.0)
        u_ref[slot] = (u * u).astype(BF16)

    def down(slot):
        return jnp.dot(u_ref[slot], wout_ref[...], preferred_element_type=F32)

    @pl.when(s == 0)
    def _():
        @pl.when(i == 0)
        def _():
            x_copy(0).start()

        x_copy(i).wait()
        x = x_buf[...]
        h_ref[...] = (x * _rms_scale(x) * g_ref[...]).astype(BF16)
        up(0)

    @pl.when(s == 1)
    def _():
        @pl.when(i > 0)
        def _():
            out_copy(i - 1).wait()

        acc_ref[...] = x_buf[...] + down(0)
        up(1)

    @pl.when(jnp.logical_and(s == 2, i + 1 < n_row_tiles))
    def _():
        x_copy(i + 1).start()

    for parity in (0, 1):
        @pl.when(jnp.logical_and(jnp.logical_and(s > 1, s < nf), s % 2 == parity))
        def _():
            acc_ref[...] += down(1 - parity)
            up(parity)

    @pl.when(s == nf)
    def _():
        acc_ref[...] += down((nf - 1) % 2)
        out_copy(i).start()

        @pl.when(i == n_row_tiles - 1)
        def _():
            out_copy(i).wait()


def _mlp(x, g, w_in, w_out, layer, *, bm=1024, bf=512):
    t, d = x.shape
    f = w_in.shape[2]
    bm, bf = min(bm, t), min(bf, f)
    nf = f // bf
    assert nf >= 2, "the x prefetch is issued at step 2"
    return pl.pallas_call(
        functools.partial(_mlp_kernel, nf=nf, bm=bm),
        grid=(t // bm, nf + 1),
        in_specs=[
            pl.BlockSpec(memory_space=pl.ANY),
            pl.BlockSpec((1, d), lambda i, s: (0, 0)),
            pl.BlockSpec((None, d, bf), lambda i, s: (layer, 0, jnp.minimum(s, nf - 1))),
            pl.BlockSpec((None, bf, d), lambda i, s: (layer, jnp.maximum(s - 1, 0), 0)),
        ],
        out_specs=pl.BlockSpec(memory_space=pl.ANY),
        out_shape=jax.ShapeDtypeStruct((t, d), F32),
        scratch_shapes=[pltpu.VMEM((bm, d), F32), pltpu.VMEM((bm, d), F32), pltpu.VMEM((bm, d), BF16),
                        pltpu.VMEM((2, bm, bf), BF16), pltpu.SemaphoreType.DMA(()), pltpu.SemaphoreType.DMA(())],
        compiler_params=_params(("arbitrary", "arbitrary")),
        name="relu2_mlp",
    )(x, g.reshape(1, d), w_in, w_out)


def _trunk(x, w):
    b, s, d = x.shape
    t = b * s
    n_heads = d // HEAD_DIM
    depth = w["norm_mix"].shape[0]
    cs, seq = _dft_tables(s)
    for layer in range(depth):
        j = layer // N_MIXERS
        if layer % N_MIXERS == 0:
            z = _chan_dft(x, w["norm_mix"][layer], cs)
            mix_in, w_o = _seq_dft(seq, z.reshape(b, DFT_RADIX, s // 2, d)), w["fourier_w_o"]
        else:
            h = _rmsnorm(x.reshape(t, d), w["norm_mix"][layer])
            bn_qkv = min(1024, d)
            qkv = _matmul(h[None], w["na_w_qkv"], bm=1024, bn=bn_qkv, bk=d, out_dtype=BF16,
                          epilogue="headnorm", extra=w["na_qk_gain"], n_norm_tiles=2 * d // bn_qkv,
                          w_layer=j, extra_layer=j, name="qkv_proj")
            mix_in, w_o = _attention(qkv.reshape(b, s, 3 * d), w["na_bias"], j), w["na_w_o"]
        x = _matmul(mix_in.reshape(1, t, d), w_o, bm=1024, bn=1024, bk=d, out_dtype=F32,
                    epilogue="residual", extra=x.reshape(1, t, d), w_layer=j, name="out_proj")
        x = _mlp(x.reshape(t, d), w["norm_ffn"][layer], w["ffn_w_in"], w["ffn_w_out"], layer).reshape(b, s, d)
    return x


def kernel(x_prompt, x_sample, norm_mix, norm_ffn, fourier_w_o, na_w_qkv, na_g_q, na_g_k, na_rpb, na_w_o,
           ffn_w_in, ffn_w_out):
    d = x_prompt.shape[-1]
    n_heads = d // HEAD_DIM
    n_na = na_w_qkv.shape[0]
    qk_gain = jnp.concatenate(
        [jnp.tile(na_g_q * (HEAD_DIM ** -0.5 * LOG2_E), (1, n_heads)), jnp.tile(na_g_k, (1, n_heads)),
         jnp.ones((n_na, d), F32)], axis=1).reshape(n_na, 1, 3 * d)
    w = {
        "norm_mix": norm_mix,
        "norm_ffn": norm_ffn,
        "fourier_w_o": fourier_w_o.astype(BF16),
        "na_w_qkv": na_w_qkv.astype(BF16),
        "na_qk_gain": qk_gain,
        "na_bias": _bias_table(na_rpb * LOG2_E),
        "na_w_o": na_w_o.astype(BF16),
        "ffn_w_in": ffn_w_in.astype(BF16),
        "ffn_w_out": ffn_w_out.astype(BF16),
    }
    return _trunk(x_prompt, w), _trunk(x_sample, w)
```

```python
import functools
import math

import numpy as np
import jax
import jax.numpy as jnp
from jax import lax
from jax.experimental import pallas as pl
from jax.experimental.pallas import tpu as pltpu

GRID_W = 64
HEAD_DIM = 128
FOURIER_GROUP = 128
WIN_ROWS = 8
WIN_COLS = 16
N_MIXERS = 2
DFT_RADIX = 4
EPS = 1e-6
NEG_INF = -1e30
LOG2_E = math.log2(math.e)

V7X_VMEM_BYTES = 64 * 1024 * 1024
VMEM_LIMIT_BYTES = V7X_VMEM_BYTES - 2 * 1024 * 1024
LANES = 128
NORM_CHUNK_ROWS = 64
ATTN_Q_ROWS = 4
ATTN_K_TILES = 3
ATTN_GROUPS_PER_STEP = 4
ATTN_HEADS_PER_STEP = 2

F32 = jnp.float32
BF16 = jnp.bfloat16


def _params(semantics):
    return pltpu.CompilerParams(dimension_semantics=semantics, vmem_limit_bytes=VMEM_LIMIT_BYTES)


def _rms_scale(x):
    return lax.rsqrt(jnp.mean(x * x, axis=-1, keepdims=True) + EPS)


def _normalise_rows(x_ref, g_ref, h_ref):
    def chunk(c, carry):
        rows = pl.ds(pl.multiple_of(c * NORM_CHUNK_ROWS, NORM_CHUNK_ROWS), NORM_CHUNK_ROWS)
        x = x_ref[rows, :]
        h_ref[rows, :] = (x * _rms_scale(x) * g_ref[...]).astype(h_ref.dtype)
        return carry

    lax.fori_loop(0, x_ref.shape[0] // NORM_CHUNK_ROWS, chunk, 0)


def _headnorm_store(acc, gain_ref, o_ref, j, n_norm_tiles):
    @pl.when(j < n_norm_tiles)
    def _():
        for h in range(acc.shape[1] // HEAD_DIM):
            sl = slice(h * HEAD_DIM, (h + 1) * HEAD_DIM)
            a = acc[:, sl]
            o_ref[:, sl] = (a * _rms_scale(a) * gain_ref[:, sl]).astype(o_ref.dtype)

    @pl.when(j >= n_norm_tiles)
    def _():
        o_ref[...] = acc.astype(o_ref.dtype)


def _norm_qkv_kernel(x_hbm, g_ref, w_ref, gain_ref, o_ref, x_buf, h0_ref, h1_ref, x_sem, *, bm, n_row_tiles,
                     n_norm_tiles):
    i, j = pl.program_id(0), pl.program_id(1)
    last_j = pl.num_programs(1) - 1

    def x_copy(tile):
        return pltpu.make_async_copy(x_hbm.at[pl.ds(tile * bm, bm)], x_buf, x_sem)

    def normalise(h_ref):
        _normalise_rows(x_buf, g_ref, h_ref)

    def project(h_ref):
        acc = jnp.dot(h_ref[...], w_ref[...], preferred_element_type=F32)
        _headnorm_store(acc, gain_ref, o_ref, j, n_norm_tiles)

    @pl.when(jnp.logical_and(i == 0, j == 0))
    def _():
        x_copy(0).start()
        x_copy(0).wait()
        normalise(h0_ref)
        if n_row_tiles > 1:
            x_copy(1).start()

    hand_over = jnp.logical_and(j == last_j, i + 1 < n_row_tiles)
    for parity, (h_cur, h_next) in enumerate(((h0_ref, h1_ref), (h1_ref, h0_ref))):
        mine = i % 2 == parity

        @pl.when(jnp.logical_and(mine, hand_over))
        def _():
            x_copy(i + 1).wait()
            normalise(h_next)

        @pl.when(mine)
        def _():
            project(h_cur)

    @pl.when(jnp.logical_and(hand_over, i + 2 < n_row_tiles))
    def _():
        x_copy(i + 2).start()


def _norm_qkv(x, g, w_qkv, gains, layer, *, bm=1024, bn=1024):
    t, d = x.shape
    n = w_qkv.shape[2]
    bm, bn = min(bm, t), min(bn, d)
    n_row_tiles = t // bm
    return pl.pallas_call(
        functools.partial(_norm_qkv_kernel, bm=bm, n_row_tiles=n_row_tiles, n_norm_tiles=2 * d // bn),
        grid=(n_row_tiles, n // bn),
        in_specs=[
            pl.BlockSpec(memory_space=pl.ANY),
            pl.BlockSpec((1, d), lambda i, j: (0, 0)),
            pl.BlockSpec((None, d, bn), lambda i, j: (layer, 0, j)),
            pl.BlockSpec((None, 1, bn), lambda i, j: (layer, 0, j)),
        ],
        out_specs=pl.BlockSpec((bm, bn), lambda i, j: (i, j)),
        out_shape=jax.ShapeDtypeStruct((t, n), BF16),
        scratch_shapes=[pltpu.VMEM((bm, d), F32), pltpu.VMEM((bm, d), BF16), pltpu.VMEM((bm, d), BF16),
                        pltpu.SemaphoreType.DMA(())],
        compiler_params=_params(("arbitrary", "arbitrary")),
        name="norm_qkv_proj",
    )(x, g.reshape(1, d), w_qkv, gains)


def _out_proj_kernel(a_ref, w_ref, x_ref, o_ref):
    o_ref[...] = x_ref[...] + jnp.dot(a_ref[...], w_ref[...], preferred_element_type=F32)


def _out_proj(a, w, layer, x, *, bm=1024, bn=1024):
    t, d = x.shape
    bm, bn = min(bm, t), min(bn, d)
    return pl.pallas_call(
        _out_proj_kernel,
        grid=(t // bm, d // bn),
        in_specs=[
            pl.BlockSpec((bm, d), lambda i, j: (i, 0)),
            pl.BlockSpec((None, d, bn), lambda i, j: (layer, 0, j)),
            pl.BlockSpec((bm, bn), lambda i, j: (i, j)),
        ],
        out_specs=pl.BlockSpec((bm, bn), lambda i, j: (i, j)),
        out_shape=jax.ShapeDtypeStruct((t, d), F32),
        compiler_params=_params(("parallel", "parallel")),
        name="out_proj",
    )(a, w, x)


def _chan_dft_kernel(x0_ref, x1_ref, x2_ref, x3_ref, g_ref, cs_ref, z_ref):
    h0, h1, h2, h3 = ((x_ref[...] * _rms_scale(x_ref[...]) * g_ref[...]) for x_ref in (x0_ref, x1_ref, x2_ref, x3_ref))
    even, odd = h0 + h2, h1 + h3
    lhs = jnp.concatenate([even + odd, even - odd, h0 - h2, h1 - h3], axis=0).astype(BF16)
    bm = h0.shape[0]
    for grp in range(h0.shape[1] // FOURIER_GROUP):
        sl = slice(grp * FOURIER_GROUP, (grp + 1) * FOURIER_GROUP)
        y = jnp.dot(lhs[:, sl], cs_ref[...], preferred_element_type=F32)
        yc, ys = y[:, :FOURIER_GROUP], y[:, FOURIER_GROUP:]
        ac, a_s, bc, b_s = yc[2 * bm:3 * bm], ys[2 * bm:3 * bm], yc[3 * bm:], ys[3 * bm:]
        z_ref[0, 0, :, sl] = yc[:bm].astype(z_ref.dtype)
        z_ref[0, 1, :, sl] = ys[:bm].astype(z_ref.dtype)
        z_ref[2, 0, :, sl] = yc[bm:2 * bm].astype(z_ref.dtype)
        z_ref[2, 1, :, sl] = ys[bm:2 * bm].astype(z_ref.dtype)
        z_ref[1, 0, :, sl] = (ac - b_s).astype(z_ref.dtype)
        z_ref[1, 1, :, sl] = (a_s + bc).astype(z_ref.dtype)
        z_ref[3, 0, :, sl] = (ac + b_s).astype(z_ref.dtype)
        z_ref[3, 1, :, sl] = (a_s - bc).astype(z_ref.dtype)


def _largest_tile(n, cap):
    tile = 16
    while tile * 2 <= cap and n % (tile * 2) == 0:
        tile *= 2
    assert n % tile == 0
    return tile


def _chan_dft(x, g, cs):
    b, s, d = x.shape
    quarter = s // DFT_RADIX
    bm = _largest_tile(quarter, 128)
    tiles = quarter // bm
    x_specs = [pl.BlockSpec((None, bm, d), functools.partial(lambda bi, i, q: (bi, q * tiles + i, 0), q=q))
               for q in range(DFT_RADIX)]
    return pl.pallas_call(
        _chan_dft_kernel,
        grid=(b, tiles),
        in_specs=x_specs + [
            pl.BlockSpec((1, d), lambda bi, i: (0, 0)),
            pl.BlockSpec((FOURIER_GROUP, 2 * FOURIER_GROUP), lambda bi, i: (0, 0)),
        ],
        out_specs=pl.BlockSpec((None, DFT_RADIX, 2, bm, d), lambda bi, i: (bi, 0, 0, i, 0)),
        out_shape=jax.ShapeDtypeStruct((b, DFT_RADIX, 2, quarter, d), BF16),
        compiler_params=_params(("parallel", "parallel")),
        name="chan_dft",
    )(x, x, x, x, g.reshape(1, d), cs)


def _seq_dft_kernel(f_ref, z_ref, o_ref, rows_ref):
    bm = f_ref.shape[1]
    for r in range(DFT_RADIX):
        y = jnp.dot(f_ref[r], z_ref[r], preferred_element_type=F32)
        for c in range(y.shape[1] // LANES):
            rows_ref[c, pl.ds(r, bm, stride=DFT_RADIX), :] = y[:, c * LANES:(c + 1) * LANES]
    for c in range(rows_ref.shape[0]):
        o_ref[:, c * LANES:(c + 1) * LANES] = rows_ref[c].astype(o_ref.dtype)


def _seq_dft(f, z):
    b, _, half, d = z.shape
    quarter = half // 2
    bm, bn = _largest_tile(quarter, 256), _largest_tile(d, 1024)
    return pl.pallas_call(
        _seq_dft_kernel,
        grid=(b, d // bn, quarter // bm),
        in_specs=[
            pl.BlockSpec((DFT_RADIX, bm, half), lambda bi, j, i: (0, i, 0)),
            pl.BlockSpec((None, DFT_RADIX, half, bn), lambda bi, j, i: (bi, 0, 0, j)),
        ],
        out_specs=pl.BlockSpec((None, DFT_RADIX * bm, bn), lambda bi, j, i: (bi, i, j)),
        out_shape=jax.ShapeDtypeStruct((b, DFT_RADIX * quarter, d), BF16),
        scratch_shapes=[pltpu.VMEM((bn // LANES, DFT_RADIX * bm, LANES), F32)],
        compiler_params=_params(("parallel", "parallel", "arbitrary")),
        name="seq_dft",
    )(f, z)


def _dft_tables(s):
    def angles(freq, n, period):
        return ((freq * n) % period).astype(F32) * (2.0 * jnp.pi / period)

    iota = lambda shape, axis: lax.broadcasted_iota(jnp.int32, shape, axis)
    grp = (FOURIER_GROUP, FOURIER_GROUP)
    ang = angles(iota(grp, 0), iota(grp, 1), FOURIER_GROUP)
    cs = jnp.concatenate([jnp.cos(ang), jnp.sin(ang)], axis=1) * FOURIER_GROUP ** -0.5
    quarter = s // DFT_RADIX
    shape = (DFT_RADIX, quarter, quarter)
    ang = angles(DFT_RADIX * iota(shape, 1) + iota(shape, 0), iota(shape, 2), s)
    seq = jnp.concatenate([jnp.cos(ang), -jnp.sin(ang)], axis=2) * s ** -0.5
    return cs.astype(BF16), seq.astype(BF16)


def _attn_kernel(q_ref, k_ref, v_ref, bias_ref, o_ref, *, n_groups, heads, groups_per_step):
    gq = ATTN_Q_ROWS * GRID_W
    gk = ATTN_K_TILES * gq

    def body(it, carry):
        groups = [it * groups_per_step + u for u in range(groups_per_step)]
        work = [(g, h) for g in groups for h in range(heads)]
        scores = []
        for g, h in work:
            lanes = slice(h * HEAD_DIM, (h + 1) * HEAD_DIM)
            k_start = pl.multiple_of(jnp.clip(g - 1, 0, n_groups - ATTN_K_TILES) * gq, gq)
            q = q_ref[pl.ds(pl.multiple_of(g * gq, gq), gq), lanes]
            k = k_ref[pl.ds(k_start, gk), lanes]
            kind = jnp.where(g == 0, 0, jnp.where(g == n_groups - 1, 2, 1))
            s = lax.dot_general(q, k, (((1,), (1,)), ((), ())), preferred_element_type=F32)
            scores.append(s + bias_ref[h, kind])
        probs, denoms = [], []
        for s in scores:
            p = jnp.exp2(s - jnp.max(s, axis=-1, keepdims=True))
            denoms.append(jnp.sum(p, axis=-1, keepdims=True))
            probs.append(p.astype(BF16))
        for (g, h), p, denom in zip(work, probs, denoms):
            lanes = slice(h * HEAD_DIM, (h + 1) * HEAD_DIM)
            k_start = pl.multiple_of(jnp.clip(g - 1, 0, n_groups - ATTN_K_TILES) * gq, gq)
            o = jnp.dot(p, v_ref[pl.ds(k_start, gk), lanes], preferred_element_type=F32)
            o_ref[pl.ds(pl.multiple_of(g * gq, gq), gq), lanes] = (o / denom).astype(o_ref.dtype)
        return carry

    lax.fori_loop(0, n_groups // groups_per_step, body, 0)


def _attention(qkv, bias, layer):
    b, s, d3 = qkv.shape
    d = d3 // 3
    heads = ATTN_HEADS_PER_STEP
    n_head_blocks = d // (heads * HEAD_DIM)
    n_groups = s // (ATTN_Q_ROWS * GRID_W)
    assert s % (ATTN_Q_ROWS * GRID_W) == 0 and n_groups >= ATTN_K_TILES
    groups_per_step = math.gcd(n_groups, ATTN_GROUPS_PER_STEP)
    width = heads * HEAD_DIM
    return pl.pallas_call(
        functools.partial(_attn_kernel, n_groups=n_groups, heads=heads, groups_per_step=groups_per_step),
        grid=(n_head_blocks, b),
        in_specs=[
            pl.BlockSpec((None, s, width), lambda h, bi: (bi, 0, h)),
            pl.BlockSpec((None, s, width), lambda h, bi: (bi, 0, n_head_blocks + h)),
            pl.BlockSpec((None, s, width), lambda h, bi: (bi, 0, 2 * n_head_blocks + h)),
            pl.BlockSpec((None, heads) + bias.shape[2:], lambda h, bi: (layer, h, 0, 0, 0)),
        ],
        out_specs=pl.BlockSpec((None, s, width), lambda h, bi: (bi, 0, h)),
        out_shape=jax.ShapeDtypeStruct((b, s, d), BF16),
        compiler_params=_params(("parallel", "parallel")),
        name="nbr_attention",
    )(qkv, qkv, qkv, bias)


def _bias_table_kernel(rpb_ref, o_ref):
    n_key_rows = ATTN_K_TILES * ATTN_Q_ROWS
    n_rel_rows = 2 * WIN_ROWS - 1
    shape = (GRID_W, LANES)
    c = lax.broadcasted_iota(jnp.int32, shape, 0)
    lane = lax.broadcasted_iota(jnp.int32, shape, 1)
    low = lane < GRID_W
    kc = lane & (GRID_W - 1)
    col_start = jnp.clip(c - WIN_COLS // 2, 0, GRID_W - WIN_COLS)
    col_ok = jnp.logical_and(kc >= col_start, kc < col_start + WIN_COLS)
    neg = jnp.full(shape, NEG_INF, F32)

    def toep(rho, extra_shift):
        row = jnp.broadcast_to(rpb_ref[rho:rho + 1, :], shape)
        return pltpu.roll(row, extra_shift, 1, stride=1, stride_axis=0)

    def pair(rho):
        lo = toep(rho, 0) if 0 <= rho < n_rel_rows else neg
        hi = toep(rho + 1, GRID_W) if 0 <= rho + 1 < n_rel_rows else neg
        return jnp.where(low, lo, hi)

    kinds = ((0, lambda t: 0), (ATTN_Q_ROWS, lambda t: t), (2 * ATTN_Q_ROWS, lambda t: ATTN_Q_ROWS))
    for kind, (first_q_row, win_start) in enumerate(kinds):
        for t in range(ATTN_Q_ROWS):
            w0 = win_start(t)
            for jp in range(n_key_rows // 2):
                ok_lo = w0 <= 2 * jp < w0 + WIN_ROWS
                ok_hi = w0 <= 2 * jp + 1 < w0 + WIN_ROWS
                if ok_lo or ok_hi:
                    rows_ok = col_ok if (ok_lo and ok_hi) else jnp.logical_and(col_ok, low if ok_lo else ~low)
                    blk = jnp.where(rows_ok, pair(2 * jp - (first_q_row + t) + WIN_ROWS - 1), neg)
                else:
                    blk = neg
                o_ref[kind, t * GRID_W:(t + 1) * GRID_W, jp * LANES:(jp + 1) * LANES] = blk


def _bias_table(rpb):
    n_layers, n_heads, n_rel_rows, n_rel_cols = rpb.shape
    rows = jnp.pad(rpb, ((0, 0), (0, 0), (0, 16 - n_rel_rows), (0, LANES - n_rel_cols)))
    rows = jnp.roll(rows, -(WIN_COLS - 1), axis=-1)
    table_shape = (3, ATTN_Q_ROWS * GRID_W, ATTN_K_TILES * ATTN_Q_ROWS * GRID_W)
    return pl.pallas_call(
        _bias_table_kernel,
        grid=(n_layers, n_heads),
        in_specs=[pl.BlockSpec((None, None, 16, LANES), lambda l, h: (l, h, 0, 0))],
        out_specs=pl.BlockSpec((None, None) + table_shape, lambda l, h: (l, h, 0, 0, 0)),
        out_shape=jax.ShapeDtypeStruct((n_layers, n_heads) + table_shape, F32),
        compiler_params=_params(("parallel", "parallel")),
        name="bias_table",
    )(rows)


def _mlp_kernel(x_hbm, g_ref, win_ref, wout_ref, o_hbm, x_buf, acc_ref, h_ref, u_ref, x_sem, o_sem, *, nf, bm):
    i, s = pl.program_id(0), pl.program_id(1)
    n_row_tiles = pl.num_programs(0)

    def x_copy(tile):
        return pltpu.make_async_copy(x_hbm.at[pl.ds(tile * bm, bm)], x_buf, x_sem)

    def out_copy(tile):
        return pltpu.make_async_copy(acc_ref, o_hbm.at[pl.ds(tile * bm, bm)], o_sem)

    def up(slot):
        u = jnp.maximum(jnp.dot(h_ref[...], win_ref[...], preferred_element_type=F32), 0.0)
        u_ref[slot] = (u * u).astype(BF16)

    def down(slot):
        return jnp.dot(u_ref[slot], wout_ref[...], preferred_element_type=F32)

    @pl.when(s == 0)
    def _():
        @pl.when(i == 0)
        def _():
            x_copy(0).start()

        x_copy(i).wait()
        x = x_buf[...]
        h_ref[...] = (x * _rms_scale(x) * g_ref[...]).astype(BF16)
        up(0)

    @pl.when(s == 1)
    def _():
        @pl.when(i > 0)
        def _():
            out_copy(i - 1).wait()

        acc_ref[...] = x_buf[...] + down(0)
        up(1)

    @pl.when(jnp.logical_and(s == 2, i + 1 < n_row_tiles))
    def _():
        x_copy(i + 1).start()

    for parity in (0, 1):
        @pl.when(jnp.logical_and(jnp.logical_and(s > 1, s < nf), s % 2 == parity))
        def _():
            acc_ref[...] += down(1 - parity)
            up(parity)

    @pl.when(s == nf)
    def _():
        acc_ref[...] += down((nf - 1) % 2)
        out_copy(i).start()

        @pl.when(i == n_row_tiles - 1)
        def _():
            out_copy(i).wait()


def _mlp(x, g, w_in, w_out, layer, *, bm=1024, bf=512):
    t, d = x.shape
    f = w_in.shape[2]
    bm, bf = min(bm, t), min(bf, f)
    nf = f // bf
    assert nf >= 2, "the x prefetch is issued at step 2"
    return pl.pallas_call(
        functools.partial(_mlp_kernel, nf=nf, bm=bm),
        grid=(t // bm, nf + 1),
        in_specs=[
            pl.BlockSpec(memory_space=pl.ANY),
            pl.BlockSpec((1, d), lambda i, s: (0, 0)),
            pl.BlockSpec((None, d, bf), lambda i, s: (layer, 0, jnp.minimum(s, nf - 1))),
            pl.BlockSpec((None, bf, d), lambda i, s: (layer, jnp.maximum(s - 1, 0), 0)),
        ],
        out_specs=pl.BlockSpec(memory_space=pl.ANY),
        out_shape=jax.ShapeDtypeStruct((t, d), F32),
        scratch_shapes=[pltpu.VMEM((bm, d), F32), pltpu.VMEM((bm, d), F32), pltpu.VMEM((bm, d), BF16),
                        pltpu.VMEM((2, bm, bf), BF16), pltpu.SemaphoreType.DMA(()), pltpu.SemaphoreType.DMA(())],
        compiler_params=_params(("arbitrary", "arbitrary")),
        name="relu2_mlp",
    )(x, g.reshape(1, d), w_in, w_out)


def _trunk(x, w):
    b, s, d = x.shape
    t = b * s
    depth = w["norm_mix"].shape[0]
    cs, seq = _dft_tables(s)
    for layer in range(depth):
        j = layer // N_MIXERS
        if layer % N_MIXERS == 0:
            z = _chan_dft(x, w["norm_mix"][layer], cs)
            mix_in, w_o = _seq_dft(seq, z.reshape(b, DFT_RADIX, s // 2, d)), w["fourier_w_o"]
        else:
            qkv = _norm_qkv(x.reshape(t, d), w["norm_mix"][layer], w["na_w_qkv"], w["na_qk_gain"], j)
            mix_in, w_o = _attention(qkv.reshape(b, s, 3 * d), w["na_bias"], j), w["na_w_o"]
        x = _out_proj(mix_in.reshape(t, d), w_o, j, x.reshape(t, d))
        x = _mlp(x, w["norm_ffn"][layer], w["ffn_w_in"], w["ffn_w_out"], layer).reshape(b, s, d)
    return x


def kernel(x_prompt, x_sample, norm_mix, norm_ffn, fourier_w_o, na_w_qkv, na_g_q, na_g_k, na_rpb, na_w_o,
           ffn_w_in, ffn_w_out):
    d = x_prompt.shape[-1]
    n_heads = d // HEAD_DIM
    n_na = na_w_qkv.shape[0]
    qk_gain = jnp.concatenate(
        [jnp.tile(na_g_q * (HEAD_DIM ** -0.5 * LOG2_E), (1, n_heads)), jnp.tile(na_g_k, (1, n_heads)),
         jnp.ones((n_na, d), F32)], axis=1).reshape(n_na, 1, 3 * d)
    w = {
        "norm_mix": norm_mix,
        "norm_ffn": norm_ffn,
        "fourier_w_o": fourier_w_o.astype(BF16),
        "na_w_qkv": na_w_qkv.astype(BF16),
        "na_qk_gain": qk_gain,
        "na_bias": _bias_table(na_rpb * LOG2_E),
        "na_w_o": na_w_o.astype(BF16),
        "ffn_w_in": ffn_w_in.astype(BF16),
        "ffn_w_out": ffn_w_out.astype(BF16),
    }
    return _trunk(x_prompt, w), _trunk(x_sample, w)
```

```python
import functools
import math

import jax
import jax.numpy as jnp
from jax import lax
from jax.experimental import pallas as pl
from jax.experimental.pallas import tpu as pltpu

GRID_W = 64
HEAD_DIM = 128
FOURIER_GROUP = 128
WIN_ROWS = 8
WIN_COLS = 16
N_MIXERS = 2
DFT_RADIX = 4
EPS = 1e-6
NEG_INF = -1e30
LOG2_E = math.log2(math.e)

V7X_VMEM_BYTES = 64 * 1024 * 1024
VMEM_LIMIT_BYTES = V7X_VMEM_BYTES - 2 * 1024 * 1024
LANES = 128
NORM_CHUNK_ROWS = 64
ATTN_Q_ROWS = 4
ATTN_K_TILES = 3
ATTN_GROUPS_PER_STEP = 4
ATTN_HEADS_PER_STEP = 2

F32 = jnp.float32
BF16 = jnp.bfloat16


def _params(semantics):
    return pltpu.CompilerParams(dimension_semantics=semantics, vmem_limit_bytes=VMEM_LIMIT_BYTES)


def _rms_scale(x):
    return lax.rsqrt(jnp.mean(x * x, axis=-1, keepdims=True) + EPS)


def _normalise_rows(x_ref, g_ref, h_ref):
    def chunk(c, carry):
        rows = pl.ds(pl.multiple_of(c * NORM_CHUNK_ROWS, NORM_CHUNK_ROWS), NORM_CHUNK_ROWS)
        x = x_ref[rows, :]
        h_ref[rows, :] = (x * _rms_scale(x) * g_ref[...]).astype(h_ref.dtype)
        return carry

    lax.fori_loop(0, x_ref.shape[0] // NORM_CHUNK_ROWS, chunk, 0)


def _norm_qkv_kernel(x_hbm, g_ref, w_ref, gain_ref, o_ref, x_buf, h0_ref, h1_ref, x_sem, *, bm, n_row_tiles,
                     n_norm_tiles):
    i, j = pl.program_id(0), pl.program_id(1)
    last_j = pl.num_programs(1) - 1

    def x_copy(tile):
        return pltpu.make_async_copy(x_hbm.at[pl.ds(tile * bm, bm)], x_buf, x_sem)

    def normalise(h_ref):
        _normalise_rows(x_buf, g_ref, h_ref)

    def project(h_ref):
        is_qk = j < n_norm_tiles
        half = w_ref.shape[1] // 2
        for c in range(2):
            acc = jnp.dot(h_ref[...], w_ref[:, c * half:(c + 1) * half], preferred_element_type=F32)
            for hd in range(half // HEAD_DIM):
                sl = slice(c * half + hd * HEAD_DIM, c * half + (hd + 1) * HEAD_DIM)
                a = acc[:, hd * HEAD_DIM:(hd + 1) * HEAD_DIM]
                scale = jnp.where(is_qk, _rms_scale(a) * gain_ref[:, sl], 1.0)
                o_ref[:, sl] = (a * scale).astype(o_ref.dtype)

    @pl.when(jnp.logical_and(i == 0, j == 0))
    def _():
        x_copy(0).start()
        x_copy(0).wait()
        normalise(h0_ref)
        if n_row_tiles > 1:
            x_copy(1).start()

    hand_over = jnp.logical_and(j == last_j, i + 1 < n_row_tiles)
    for parity, (h_cur, h_next) in enumerate(((h0_ref, h1_ref), (h1_ref, h0_ref))):
        mine = i % 2 == parity

        @pl.when(jnp.logical_and(mine, hand_over))
        def _():
            x_copy(i + 1).wait()
            normalise(h_next)

        @pl.when(mine)
        def _():
            project(h_cur)

    @pl.when(jnp.logical_and(hand_over, i + 2 < n_row_tiles))
    def _():
        x_copy(i + 2).start()


def _norm_qkv(x, g, w_qkv, gains, layer, *, bm=1024, bn=1024):
    t, d = x.shape
    n = w_qkv.shape[2]
    bm, bn = min(bm, t), min(bn, d)
    n_row_tiles = t // bm
    return pl.pallas_call(
        functools.partial(_norm_qkv_kernel, bm=bm, n_row_tiles=n_row_tiles, n_norm_tiles=2 * d // bn),
        grid=(n_row_tiles, n // bn),
        in_specs=[
            pl.BlockSpec(memory_space=pl.ANY),
            pl.BlockSpec((1, d), lambda i, j: (0, 0)),
            pl.BlockSpec((None, d, bn), lambda i, j: (layer, 0, j)),
            pl.BlockSpec((None, 1, bn), lambda i, j: (layer, 0, j)),
        ],
        out_specs=pl.BlockSpec((bm, bn), lambda i, j: (i, j)),
        out_shape=jax.ShapeDtypeStruct((t, n), BF16),
        scratch_shapes=[pltpu.VMEM((bm, d), F32), pltpu.VMEM((bm, d), BF16), pltpu.VMEM((bm, d), BF16),
                        pltpu.SemaphoreType.DMA(())],
        compiler_params=_params(("arbitrary", "arbitrary")),
        name="norm_qkv_proj",
    )(x, g.reshape(1, d), w_qkv, gains)


def _out_proj_kernel(a_ref, w_ref, x_ref, o_ref):
    o_ref[...] = x_ref[...] + jnp.dot(a_ref[...], w_ref[...], preferred_element_type=F32)


def _out_proj(a, w, layer, x, *, bm=1024, bn=1024):
    t, d = x.shape
    bm, bn = min(bm, t), min(bn, d)
    return pl.pallas_call(
        _out_proj_kernel,
        grid=(t // bm, d // bn),
        in_specs=[
            pl.BlockSpec((bm, d), lambda i, j: (i, 0)),
            pl.BlockSpec((None, d, bn), lambda i, j: (layer, 0, j)),
            pl.BlockSpec((bm, bn), lambda i, j: (i, j)),
        ],
        out_specs=pl.BlockSpec((bm, bn), lambda i, j: (i, j)),
        out_shape=jax.ShapeDtypeStruct((t, d), F32),
        compiler_params=_params(("parallel", "parallel")),
        name="out_proj",
    )(a, w, x)


def _chan_dft_kernel(x0_ref, x1_ref, x2_ref, x3_ref, g_ref, cs_ref, z_ref):
    h0, h1, h2, h3 = ((x_ref[...] * _rms_scale(x_ref[...]) * g_ref[...]) for x_ref in (x0_ref, x1_ref, x2_ref, x3_ref))
    even, odd = h0 + h2, h1 + h3
    lhs = jnp.concatenate([even + odd, even - odd, h0 - h2, h1 - h3], axis=0).astype(BF16)
    bm = h0.shape[0]
    for grp in range(h0.shape[1] // FOURIER_GROUP):
        sl = slice(grp * FOURIER_GROUP, (grp + 1) * FOURIER_GROUP)
        y = jnp.dot(lhs[:, sl], cs_ref[...], preferred_element_type=F32)
        yc, ys = y[:, :FOURIER_GROUP], y[:, FOURIER_GROUP:]
        ac, a_s, bc, b_s = yc[2 * bm:3 * bm], ys[2 * bm:3 * bm], yc[3 * bm:], ys[3 * bm:]
        z_ref[0, 0, :, sl] = yc[:bm].astype(z_ref.dtype)
        z_ref[0, 1, :, sl] = ys[:bm].astype(z_ref.dtype)
        z_ref[2, 0, :, sl] = yc[bm:2 * bm].astype(z_ref.dtype)
        z_ref[2, 1, :, sl] = ys[bm:2 * bm].astype(z_ref.dtype)
        z_ref[1, 0, :, sl] = (ac - b_s).astype(z_ref.dtype)
        z_ref[1, 1, :, sl] = (a_s + bc).astype(z_ref.dtype)
        z_ref[3, 0, :, sl] = (ac + b_s).astype(z_ref.dtype)
        z_ref[3, 1, :, sl] = (a_s - bc).astype(z_ref.dtype)


def _largest_tile(n, cap):
    tile = 16
    while tile * 2 <= cap and n % (tile * 2) == 0:
        tile *= 2
    assert n % tile == 0
    return tile


def _chan_dft(x, g, cs):
    b, s, d = x.shape
    quarter = s // DFT_RADIX
    bm = _largest_tile(quarter, 128)
    tiles = quarter // bm
    x_specs = [pl.BlockSpec((None, bm, d), functools.partial(lambda bi, i, q: (bi, q * tiles + i, 0), q=q))
               for q in range(DFT_RADIX)]
    return pl.pallas_call(
        _chan_dft_kernel,
        grid=(b, tiles),
        in_specs=x_specs + [
            pl.BlockSpec((1, d), lambda bi, i: (0, 0)),
            pl.BlockSpec((FOURIER_GROUP, 2 * FOURIER_GROUP), lambda bi, i: (0, 0)),
        ],
        out_specs=pl.BlockSpec((None, DFT_RADIX, 2, bm, d), lambda bi, i: (bi, 0, 0, i, 0)),
        out_shape=jax.ShapeDtypeStruct((b, DFT_RADIX, 2, quarter, d), BF16),
        compiler_params=_params(("parallel", "parallel")),
        name="chan_dft",
    )(x, x, x, x, g.reshape(1, d), cs)


def _seq_dft_kernel(f_ref, z_ref, o_ref, rows_ref):
    bm = f_ref.shape[1]
    for r in range(DFT_RADIX):
        y = jnp.dot(f_ref[r], z_ref[r], preferred_element_type=F32)
        for c in range(y.shape[1] // LANES):
            rows_ref[c, pl.ds(r, bm, stride=DFT_RADIX), :] = y[:, c * LANES:(c + 1) * LANES]
    for c in range(rows_ref.shape[0]):
        o_ref[:, c * LANES:(c + 1) * LANES] = rows_ref[c].astype(o_ref.dtype)


def _seq_dft(f, z):
    b, _, half, d = z.shape
    quarter = half // 2
    bm, bn = _largest_tile(quarter, 256), _largest_tile(d, 1024)
    return pl.pallas_call(
        _seq_dft_kernel,
        grid=(b, d // bn, quarter // bm),
        in_specs=[
            pl.BlockSpec((DFT_RADIX, bm, half), lambda bi, j, i: (0, i, 0)),
            pl.BlockSpec((None, DFT_RADIX, half, bn), lambda bi, j, i: (bi, 0, 0, j)),
        ],
        out_specs=pl.BlockSpec((None, DFT_RADIX * bm, bn), lambda bi, j, i: (bi, i, j)),
        out_shape=jax.ShapeDtypeStruct((b, DFT_RADIX * quarter, d), BF16),
        scratch_shapes=[pltpu.VMEM((bn // LANES, DFT_RADIX * bm, LANES), F32)],
        compiler_params=_params(("parallel", "parallel", "arbitrary")),
        name="seq_dft",
    )(f, z)


def _dft_tables(s):
    def angles(freq, n, period):
        return ((freq * n) % period).astype(F32) * (2.0 * jnp.pi / period)

    iota = lambda shape, axis: lax.broadcasted_iota(jnp.int32, shape, axis)
    grp = (FOURIER_GROUP, FOURIER_GROUP)
    ang = angles(iota(grp, 0), iota(grp, 1), FOURIER_GROUP)
    cs = jnp.concatenate([jnp.cos(ang), jnp.sin(ang)], axis=1) * FOURIER_GROUP ** -0.5
    quarter = s // DFT_RADIX
    shape = (DFT_RADIX, quarter, quarter)
    ang = angles(DFT_RADIX * iota(shape, 1) + iota(shape, 0), iota(shape, 2), s)
    seq = jnp.concatenate([jnp.cos(ang), -jnp.sin(ang)], axis=2) * s ** -0.5
    return cs.astype(BF16), seq.astype(BF16)


def _attn_kernel(q_ref, k_ref, v_ref, bias_ref, o_ref, *, n_groups, heads, groups_per_step):
    gq = ATTN_Q_ROWS * GRID_W
    gk = ATTN_K_TILES * gq

    def body(it, carry):
        groups = [it * groups_per_step + u for u in range(groups_per_step)]
        work = [(g, h) for g in groups for h in range(heads)]
        scores = []
        for g, h in work:
            lanes = slice(h * HEAD_DIM, (h + 1) * HEAD_DIM)
            k_start = pl.multiple_of(jnp.clip(g - 1, 0, n_groups - ATTN_K_TILES) * gq, gq)
            q = q_ref[pl.ds(pl.multiple_of(g * gq, gq), gq), lanes]
            k = k_ref[pl.ds(k_start, gk), lanes]
            kind = jnp.where(g == 0, 0, jnp.where(g == n_groups - 1, 2, 1))
            s = lax.dot_general(q, k, (((1,), (1,)), ((), ())), preferred_element_type=F32)
            scores.append(s + bias_ref[h, kind])
        probs, denoms = [], []
        for s in scores:
            p = jnp.exp2(s - jnp.max(s, axis=-1, keepdims=True))
            denoms.append(jnp.sum(p, axis=-1, keepdims=True))
            probs.append(p.astype(BF16))
        for (g, h), p, denom in zip(work, probs, denoms):
            lanes = slice(h * HEAD_DIM, (h + 1) * HEAD_DIM)
            k_start = pl.multiple_of(jnp.clip(g - 1, 0, n_groups - ATTN_K_TILES) * gq, gq)
            o = jnp.dot(p, v_ref[pl.ds(k_start, gk), lanes], preferred_element_type=F32)
            o_ref[pl.ds(pl.multiple_of(g * gq, gq), gq), lanes] = (o / denom).astype(o_ref.dtype)
        return carry

    lax.fori_loop(0, n_groups // groups_per_step, body, 0)


def _attention(qkv, bias, layer):
    b, s, d3 = qkv.shape
    d = d3 // 3
    heads = ATTN_HEADS_PER_STEP
    n_head_blocks = d // (heads * HEAD_DIM)
    n_groups = s // (ATTN_Q_ROWS * GRID_W)
    assert s % (ATTN_Q_ROWS * GRID_W) == 0 and n_groups >= ATTN_K_TILES
    groups_per_step = math.gcd(n_groups, ATTN_GROUPS_PER_STEP)
    width = heads * HEAD_DIM
    return pl.pallas_call(
        functools.partial(_attn_kernel, n_groups=n_groups, heads=heads, groups_per_step=groups_per_step),
        grid=(n_head_blocks, b),
        in_specs=[
            pl.BlockSpec((None, s, width), lambda h, bi: (bi, 0, h)),
            pl.BlockSpec((None, s, width), lambda h, bi: (bi, 0, n_head_blocks + h)),
            pl.BlockSpec((None, s, width), lambda h, bi: (bi, 0, 2 * n_head_blocks + h)),
            pl.BlockSpec((None, heads) + bias.shape[2:], lambda h, bi: (layer, h, 0, 0, 0)),
        ],
        out_specs=pl.BlockSpec((None, s, width), lambda h, bi: (bi, 0, h)),
        out_shape=jax.ShapeDtypeStruct((b, s, d), BF16),
        compiler_params=_params(("parallel", "parallel")),
        name="nbr_attention",
    )(qkv, qkv, qkv, bias)


def _bias_table_kernel(rpb_ref, o_ref):
    n_key_rows = ATTN_K_TILES * ATTN_Q_ROWS
    n_rel_rows = 2 * WIN_ROWS - 1
    shape = (GRID_W, LANES)
    c = lax.broadcasted_iota(jnp.int32, shape, 0)
    lane = lax.broadcasted_iota(jnp.int32, shape, 1)
    low = lane < GRID_W
    kc = lane & (GRID_W - 1)
    col_start = jnp.clip(c - WIN_COLS // 2, 0, GRID_W - WIN_COLS)
    col_ok = jnp.logical_and(kc >= col_start, kc < col_start + WIN_COLS)
    neg = jnp.full(shape, NEG_INF, F32)

    def toep(rho, extra_shift):
        row = jnp.broadcast_to(rpb_ref[rho:rho + 1, :], shape)
        return pltpu.roll(row, extra_shift, 1, stride=1, stride_axis=0)

    def pair(rho):
        lo = toep(rho, 0) if 0 <= rho < n_rel_rows else neg
        hi = toep(rho + 1, GRID_W) if 0 <= rho + 1 < n_rel_rows else neg
        return jnp.where(low, lo, hi)

    kinds = ((0, lambda t: 0), (ATTN_Q_ROWS, lambda t: t), (2 * ATTN_Q_ROWS, lambda t: ATTN_Q_ROWS))
    for kind, (first_q_row, win_start) in enumerate(kinds):
        for t in range(ATTN_Q_ROWS):
            w0 = win_start(t)
            for jp in range(n_key_rows // 2):
                ok_lo = w0 <= 2 * jp < w0 + WIN_ROWS
                ok_hi = w0 <= 2 * jp + 1 < w0 + WIN_ROWS
                if ok_lo or ok_hi:
                    rows_ok = col_ok if (ok_lo and ok_hi) else jnp.logical_and(col_ok, low if ok_lo else ~low)
                    blk = jnp.where(rows_ok, pair(2 * jp - (first_q_row + t) + WIN_ROWS - 1), neg)
                else:
                    blk = neg
                o_ref[kind, t * GRID_W:(t + 1) * GRID_W, jp * LANES:(jp + 1) * LANES] = blk


def _bias_table(rpb):
    n_layers, n_heads, n_rel_rows, n_rel_cols = rpb.shape
    rows = jnp.pad(rpb, ((0, 0), (0, 0), (0, 16 - n_rel_rows), (0, LANES - n_rel_cols)))
    rows = jnp.roll(rows, -(WIN_COLS - 1), axis=-1)
    table_shape = (3, ATTN_Q_ROWS * GRID_W, ATTN_K_TILES * ATTN_Q_ROWS * GRID_W)
    return pl.pallas_call(
        _bias_table_kernel,
        grid=(n_layers, n_heads),
        in_specs=[pl.BlockSpec((None, None, 16, LANES), lambda l, h: (l, h, 0, 0))],
        out_specs=pl.BlockSpec((None, None) + table_shape, lambda l, h: (l, h, 0, 0, 0)),
        out_shape=jax.ShapeDtypeStruct((n_layers, n_heads) + table_shape, F32),
        compiler_params=_params(("parallel", "parallel")),
        name="bias_table",
    )(rows)


def _mlp_kernel(x_hbm, g_ref, win_ref, wout_ref, o_hbm, x_buf, acc_ref, h_ref, u_ref, x_sem, o_sem, *, nf, bm):
    i, s = pl.program_id(0), pl.program_id(1)
    n_row_tiles = pl.num_programs(0)

    def x_copy(tile):
        return pltpu.make_async_copy(x_hbm.at[pl.ds(tile * bm, bm)], x_buf, x_sem)

    def out_copy(tile):
        return pltpu.make_async_copy(acc_ref, o_hbm.at[pl.ds(tile * bm, bm)], o_sem)

    def up(slot):
        u = jnp.maximum(jnp.dot(h_ref[...], win_ref[...], preferred_element_type=F32), 0.0)
        u_ref[slot] = (u * u).astype(BF16)

    def down(slot):
        return jnp.dot(u_ref[slot], wout_ref[...], preferred_element_type=F32)

    @pl.when(s == 0)
    def _():
        @pl.when(i == 0)
        def _():
            x_copy(0).start()

        x_copy(i).wait()
        x = x_buf[...]
        h_ref[...] = (x * _rms_scale(x) * g_ref[...]).astype(BF16)
        up(0)

    @pl.when(s == 1)
    def _():
        @pl.when(i > 0)
        def _():
            out_copy(i - 1).wait()

        acc_ref[...] = x_buf[...] + down(0)
        up(1)

    @pl.when(jnp.logical_and(s == 2, i + 1 < n_row_tiles))
    def _():
        x_copy(i + 1).start()

    for parity in (0, 1):
        @pl.when(jnp.logical_and(jnp.logical_and(s > 1, s < nf), s % 2 == parity))
        def _():
            acc_ref[...] += down(1 - parity)
            up(parity)

    @pl.when(s == nf)
    def _():
        acc_ref[...] += down((nf - 1) % 2)
        out_copy(i).start()

        @pl.when(i == n_row_tiles - 1)
        def _():
            out_copy(i).wait()


def _mlp(x, g, w_in, w_out, layer, *, bm=1024, bf=512):
    t, d = x.shape
    f = w_in.shape[2]
    bm, bf = min(bm, t), min(bf, f)
    nf = f // bf
    assert nf >= 2, "the x prefetch is issued at step 2"
    return pl.pallas_call(
        functools.partial(_mlp_kernel, nf=nf, bm=bm),
        grid=(t // bm, nf + 1),
        in_specs=[
            pl.BlockSpec(memory_space=pl.ANY),
            pl.BlockSpec((1, d), lambda i, s: (0, 0)),
            pl.BlockSpec((None, d, bf), lambda i, s: (layer, 0, jnp.minimum(s, nf - 1))),
            pl.BlockSpec((None, bf, d), lambda i, s: (layer, jnp.maximum(s - 1, 0), 0)),
        ],
        out_specs=pl.BlockSpec(memory_space=pl.ANY),
        out_shape=jax.ShapeDtypeStruct((t, d), F32),
        scratch_shapes=[pltpu.VMEM((bm, d), F32), pltpu.VMEM((bm, d), F32), pltpu.VMEM((bm, d), BF16),
                        pltpu.VMEM((2, bm, bf), BF16), pltpu.SemaphoreType.DMA(()), pltpu.SemaphoreType.DMA(())],
        compiler_params=_params(("arbitrary", "arbitrary")),
        name="relu2_mlp",
    )(x, g.reshape(1, d), w_in, w_out)


def _trunk(x, w):
    b, s, d = x.shape
    t = b * s
    depth = w["norm_mix"].shape[0]
    cs, seq = _dft_tables(s)
    for layer in range(depth):
        j = layer // N_MIXERS
        if layer % N_MIXERS == 0:
            z = _chan_dft(x, w["norm_mix"][layer], cs)
            mix_in, w_o = _seq_dft(seq, z.reshape(b, DFT_RADIX, s // 2, d)), w["fourier_w_o"]
        else:
            qkv = _norm_qkv(x.reshape(t, d), w["norm_mix"][layer], w["na_w_qkv"], w["na_qk_gain"], j)
            mix_in, w_o = _attention(qkv.reshape(b, s, 3 * d), w["na_bias"], j), w["na_w_o"]
        x = _out_proj(mix_in.reshape(t, d), w_o, j, x.reshape(t, d))
        x = _mlp(x, w["norm_ffn"][layer], w["ffn_w_in"], w["ffn_w_out"], layer).reshape(b, s, d)
    return x


def kernel(x_prompt, x_sample, norm_mix, norm_ffn, fourier_w_o, na_w_qkv, na_g_q, na_g_k, na_rpb, na_w_o,
           ffn_w_in, ffn_w_out):
    d = x_prompt.shape[-1]
    n_heads = d // HEAD_DIM
    n_na = na_w_qkv.shape[0]
    qk_gain = jnp.concatenate(
        [jnp.tile(na_g_q * (HEAD_DIM ** -0.5 * LOG2_E), (1, n_heads)), jnp.tile(na_g_k, (1, n_heads)),
         jnp.ones((n_na, d), F32)], axis=1).reshape(n_na, 1, 3 * d)
    w = {
        "norm_mix": norm_mix,
        "norm_ffn": norm_ffn,
        "fourier_w_o": fourier_w_o.astype(BF16),
        "na_w_qkv": na_w_qkv.astype(BF16),
        "na_qk_gain": qk_gain,
        "na_bias": _bias_table(na_rpb * LOG2_E),
        "na_w_o": na_w_o.astype(BF16),
        "ffn_w_in": ffn_w_in.astype(BF16),
        "ffn_w_out": ffn_w_out.astype(BF16),
    }
    return _trunk(x_prompt, w), _trunk(x_sample, w)
```
